```python
import math, functools
import jax, jax.numpy as jnp
from jax import lax
import numpy as np

D_MODEL = 2048
BATCH = 1
SEQ = 8192
DEPTH = 2
DEC_BATCH = 32
DEC_SEQ = 8
PAST_LEN = 8192
PAGE_SIZE = 128

GROUP_W = D_MODEL // 4
H_A = 4
DH_A = GROUP_W // (2 * H_A)
VH_A = 2 * DH_A
H_B = 4
DH_B = GROUP_W // H_B
MOBA_BLOCK = 256
MOBA_TOPK = 3
C_C = GROUP_W
C_GROUPS = 4
CONV_W = 31
H_D = 4
DH_D = GROUP_W // H_D
IN_SPLITS = (H_A * 2 * DH_A, H_A * 2 * DH_A, H_A * VH_A, H_B * DH_B, H_B * DH_B, H_B * DH_B, 2 * C_C, H_D * DH_D, H_D * DH_D, H_D * DH_D)
D_IN = sum(IN_SPLITS)
D_MIX = H_A * VH_A + H_B * DH_B + C_C + H_D * DH_D
D_FF = -(-8 * D_MODEL // (3 * 256)) * 256
Q_BLOCK = 128
POOL_NUM = 5
POOL_DEN = 4
EPS = 1e-6

kernel_name = 'hybrid_diff_moba_conformer_stickbreak_step'


def rms_norm(x, g, eps=EPS):
    xf = x.astype(jnp.float32)
    y = xf * lax.rsqrt(jnp.mean(xf * xf, axis=-1, keepdims=True) + eps) * g.astype(jnp.float32)
    return y.astype(x.dtype)


def over_query_blocks(fn, qs, q_pos):
    s = q_pos.shape[0]
    if s <= Q_BLOCK:
        return fn(qs, q_pos)
    nblk = s // Q_BLOCK

    def split(a):
        return jnp.moveaxis(a.reshape((a.shape[0], nblk, Q_BLOCK) + a.shape[2:]), 1, 0)

    out = lax.map(lambda xs: fn(xs[0], xs[1]), (tuple(split(q) for q in qs), q_pos.reshape(nblk, Q_BLOCK)))
    out = jnp.moveaxis(out, 0, 1)
    return out.reshape((out.shape[0], s) + out.shape[3:])


def diff_attention(q1, q2, k1, k2, v, q_pos, k_pos, lam):
    scale = DH_A ** -0.5
    causal = k_pos[None, :] <= q_pos[:, None]

    def probs(qi, ki):
        sc = jnp.einsum('bqhd,bkhd->bhqk', qi, ki).astype(jnp.float32) * scale
        return jax.nn.softmax(jnp.where(causal, sc, -jnp.inf), axis=-1)

    w = probs(q1, k1) - lam * probs(q2, k2)
    return jnp.einsum('bhqk,bkhd->bqhd', w.astype(v.dtype), v)


def moba_attention(q, k_blk, v_blk, k_mean, q_pos):
    b = q.shape[0]
    nb = k_mean.shape[2]
    own = q_pos // MOBA_BLOCK
    gate = jnp.einsum('bqhd,bhnd->bhqn', q, k_mean).astype(jnp.float32)
    fully_past = jnp.arange(nb)[None, :] < own[:, None]
    gate = jnp.where(fully_past, gate, -jnp.inf)
    top_s, top_i = lax.top_k(gate, min(MOBA_TOPK, nb))
    own_i = jnp.broadcast_to(own[None, None, :, None], top_i.shape[:3] + (1,)).astype(top_i.dtype)
    idx = jnp.concatenate([top_i, own_i], axis=-1)
    ok = jnp.concatenate([jnp.isfinite(top_s), jnp.ones(own_i.shape, dtype=bool)], axis=-1)
    bi = jnp.arange(b)[:, None, None, None]
    hi = jnp.arange(H_B)[None, :, None, None]
    kg = k_blk[bi, hi, idx]
    vg = v_blk[bi, hi, idx]
    kpos = idx[..., None] * MOBA_BLOCK + jnp.arange(MOBA_BLOCK, dtype=idx.dtype)
    mask = ok[..., None] & (kpos <= q_pos[None, None, :, None, None])
    sc = jnp.einsum('bqhd,bhqnmd->bhqnm', q, kg).astype(jnp.float32) * (DH_B ** -0.5)
    sc = jnp.where(mask, sc, -jnp.inf)
    p = jax.nn.softmax(sc.reshape(sc.shape[:3] + (-1,)), axis=-1).reshape(sc.shape)
    return jnp.einsum('bhqnm,bhqnmd->bqhd', p.astype(vg.dtype), vg)


def stick_breaking(q, k, v, q_pos, k_pos):
    z = jnp.einsum('bqhd,bkhd->bhqk', q, k).astype(jnp.float32) * (DH_D ** -0.5)
    strict = k_pos[None, :] < q_pos[:, None]
    log_keep = jnp.where(strict, jax.nn.log_sigmoid(-z), 0.0)
    later = lax.cumsum(log_keep, axis=3, reverse=True) - log_keep
    w = jnp.where(strict, jnp.exp(jax.nn.log_sigmoid(z) + later), 0.0)
    return jnp.einsum('bhqk,bkhd->bqhd', w.astype(v.dtype), v)


def layer(x, li, pos0, past_rows, conv_buf, prm):
    (g_mix, w_in, lam_a, g_a_sub, b_c_glu, w_c_dw, b_c_dw, g_c_norm, b_c_norm,
     w_out, g_ffn, w_gate, w_up, w_down) = prm
    b, s, _ = x.shape
    h = rms_norm(x, g_mix)
    cuts = np.cumsum(IN_SPLITS)[:-1].tolist()
    qa, ka, va, qb, kb, vb, glu, qd, kd, vd = jnp.split(h @ w_in, cuts, axis=-1)

    def heads(t, n):
        return t.reshape(b, s, n, -1)

    qa, ka, va = heads(qa, H_A), heads(ka, H_A), heads(va, H_A)
    qb, kb, vb = heads(qb, H_B), heads(kb, H_B), heads(vb, H_B)
    qd, kd, vd = heads(qd, H_D), heads(kd, H_D), heads(vd, H_D)

    def with_past(i, new):
        return jnp.concatenate([past_rows(i), new], axis=1)

    ka_all, va_all = with_past(0, ka), with_past(1, va)
    kb_all, vb_all = with_past(2, kb), with_past(3, vb)
    kd_all, vd_all = with_past(4, kd), with_past(5, vd)
    n_keys = ka_all.shape[1]
    q_pos = pos0 + jnp.arange(s, dtype=jnp.int32)
    k_pos = jnp.arange(n_keys, dtype=jnp.int32)

    lam_init = 0.8 - 0.6 * math.exp(-0.3 * li)
    lp = lam_a.astype(jnp.float32)
    lam = jnp.exp(jnp.sum(lp[0] * lp[1])) - jnp.exp(jnp.sum(lp[2] * lp[3])) + lam_init
    ka1, ka2 = ka_all[..., :DH_A], ka_all[..., DH_A:]
    o_a = over_query_blocks(
        lambda qs, qp: diff_attention(qs[0], qs[1], ka1, ka2, va_all, qp, k_pos, lam),
        (qa[..., :DH_A], qa[..., DH_A:]), q_pos)
    o_a = rms_norm(o_a, g_a_sub) * (1.0 - lam_init)

    nb = -(-n_keys // MOBA_BLOCK)
    pad = ((0, 0), (0, nb * MOBA_BLOCK - n_keys), (0, 0), (0, 0))

    def blocks(t):
        return jnp.pad(t, pad).reshape(b, nb, MOBA_BLOCK, H_B, DH_B).transpose(0, 3, 1, 2, 4)

    kb_blk, vb_blk = blocks(kb_all), blocks(vb_all)
    kb_mean = jnp.mean(kb_blk.astype(jnp.float32), axis=3).astype(kb_blk.dtype)
    o_b = over_query_blocks(lambda qs, qp: moba_attention(qs[0], kb_blk, vb_blk, kb_mean, qp), (qb,), q_pos)

    a_c, g_c = jnp.split(glu + b_c_glu, 2, axis=-1)
    u = a_c * jax.nn.sigmoid(g_c)
    buf = jnp.concatenate([conv_buf, u], axis=1)
    y = lax.conv_general_dilated(buf, w_c_dw[:, None, :], window_strides=(1,), padding='VALID',
                                 dimension_numbers=('NWC', 'WIO', 'NWC'), feature_group_count=C_C) + b_c_dw
    yf = y.astype(jnp.float32).reshape(b, s, C_GROUPS, C_C // C_GROUPS)
    mu = jnp.mean(yf, axis=-1, keepdims=True)
    var = jnp.mean(jnp.square(yf - mu), axis=-1, keepdims=True)
    yn = ((yf - mu) * lax.rsqrt(var + EPS)).reshape(b, s, C_C) * g_c_norm.astype(jnp.float32) + b_c_norm.astype(jnp.float32)
    o_c = jax.nn.silu(yn).astype(x.dtype)
    new_buf = buf[:, buf.shape[1] - (CONV_W - 1):]

    o_d = over_query_blocks(lambda qs, qp: stick_breaking(qs[0], kd_all, vd_all, qp, k_pos), (qd,), q_pos)

    mix = jnp.concatenate([o_a.reshape(b, s, -1), o_b.reshape(b, s, -1), o_c, o_d.reshape(b, s, -1)], axis=-1)
    x = x + mix @ w_out
    h2 = rms_norm(x, g_ffn)
    x = x + (jax.nn.silu(h2 @ w_gate) * (h2 @ w_up)) @ w_down
    return x, (ka, va, kb, vb, kd, vd, new_buf)


def setup_inputs(seed: int = 0) -> dict:
    key = jax.random.key(seed)
    ks = jax.random.split(key, 26)
    n_pages = PAST_LEN // PAGE_SIZE
    n_pool = DEC_BATCH * n_pages * POOL_NUM // POOL_DEN

    def nrm(k, shape, scale=1.0):
        return jax.random.normal(k, shape, jnp.float32) * scale

    def gain(k, shape):
        return 1.0 + nrm(k, shape, 0.02)

    pool = (DEPTH, n_pool, PAGE_SIZE)
    page_table = jax.random.permutation(ks[9], n_pool)[:DEC_BATCH * n_pages].reshape(DEC_BATCH, n_pages).astype(jnp.int32)
    return {
        'x_prompt': nrm(ks[0], (BATCH, SEQ, D_MODEL)),
        'x_sample': nrm(ks[1], (DEC_BATCH, DEC_SEQ, D_MODEL)),
        'cache_a_k': nrm(ks[2], pool + (H_A, 2 * DH_A)),
        'cache_a_v': nrm(ks[3], pool + (H_A, VH_A)),
        'cache_b_k': nrm(ks[4], pool + (H_B, DH_B)),
        'cache_b_v': nrm(ks[5], pool + (H_B, DH_B)),
        'cache_d_k': nrm(ks[6], pool + (H_D, DH_D)),
        'cache_d_v': nrm(ks[7], pool + (H_D, DH_D)),
        'state_c_conv': nrm(ks[8], (DEPTH, DEC_BATCH, CONV_W - 1, C_C), 0.5),
        'page_table': page_table,
        'g_mix': gain(ks[10], (DEPTH, D_MODEL)),
        'w_in': nrm(ks[11], (DEPTH, D_MODEL, D_IN), D_MODEL ** -0.5),
        'lam_a': nrm(ks[12], (DEPTH, 4, DH_A), 0.1),
        'g_a_sub': gain(ks[13], (DEPTH, VH_A)),
        'b_c_glu': nrm(ks[14], (DEPTH, 2 * C_C), 0.02),
        'w_c_dw': nrm(ks[15], (DEPTH, CONV_W, C_C), CONV_W ** -0.5),
        'b_c_dw': nrm(ks[16], (DEPTH, C_C), 0.02),
        'g_c_norm': gain(ks[17], (DEPTH, C_C)),
        'b_c_norm': nrm(ks[18], (DEPTH, C_C), 0.02),
        'w_out': nrm(ks[19], (DEPTH, D_MIX, D_MODEL), D_MIX ** -0.5),
        'g_ffn': gain(ks[20], (DEPTH, D_MODEL)),
        'w_gate': nrm(ks[21], (DEPTH, D_MODEL, D_FF), D_MODEL ** -0.5),
        'w_up': nrm(ks[22], (DEPTH, D_MODEL, D_FF), D_MODEL ** -0.5),
        'w_down': nrm(ks[23], (DEPTH, D_FF, D_MODEL), D_FF ** -0.5),
        'g_final': gain(ks[24], (D_MODEL,)),
    }


def reference(x_prompt, x_sample, cache_a_k, cache_a_v, cache_b_k, cache_b_v, cache_d_k, cache_d_v,
              state_c_conv, page_table, g_mix, w_in, lam_a, g_a_sub, b_c_glu, w_c_dw, b_c_dw,
              g_c_norm, b_c_norm, w_out, g_ffn, w_gate, w_up, w_down, g_final):
    caches = (cache_a_k, cache_a_v, cache_b_k, cache_b_v, cache_d_k, cache_d_v)
    n_prompt = x_prompt.shape[0]
    n_dec, n_pages = page_table.shape
    past_len = n_pages * cache_a_k.shape[2]

    def empty_rows(i):
        return jnp.zeros((n_prompt, 0) + caches[i].shape[3:], x_prompt.dtype)

    def paged_rows(li, i):
        return caches[i][li, page_table].reshape((n_dec, past_len) + caches[i].shape[3:])

    yp, ys = x_prompt, x_sample
    new_p = [[] for _ in range(7)]
    new_s = [[] for _ in range(7)]
    for li in range(DEPTH):
        prm = (g_mix[li], w_in[li], lam_a[li], g_a_sub[li], b_c_glu[li], w_c_dw[li], b_c_dw[li],
               g_c_norm[li], b_c_norm[li], w_out[li], g_ffn[li], w_gate[li], w_up[li], w_down[li])
        yp, st_p = layer(yp, li, 0, empty_rows, jnp.zeros((n_prompt, CONV_W - 1, C_C), yp.dtype), prm)
        ys, st_s = layer(ys, li, past_len, functools.partial(paged_rows, li), state_c_conv[li], prm)
        for acc, val in zip(new_p, st_p):
            acc.append(val)
        for acc, val in zip(new_s, st_s):
            acc.append(val)

    p_ak, p_av, p_bk, p_bv, p_dk, p_dv, p_c = [jnp.stack(v, axis=0) for v in new_p]
    s_ak, s_av, s_bk, s_bv, s_dk, s_dv, s_c = [jnp.stack(v, axis=0) for v in new_s]
    y_prompt = rms_norm(yp, g_final)
    y_sample = rms_norm(ys, g_final)
    return (y_prompt, y_sample, p_ak, p_av, p_bk, p_bv, p_dk, p_dv, p_c, s_ak, s_av, s_bk, s_bv, s_dk, s_dv, s_c)
```

```python
import functools
import math

import jax
import jax.numpy as jnp
from jax import lax
from jax.experimental import pallas as pl
from jax.experimental.pallas import tpu as pltpu

F32 = jnp.float32
BF16 = jnp.bfloat16

EPS = 1e-6
N_HEADS = 4
DH = 128
GROUP_W = N_HEADS * DH
DH_A = DH // 2
MOBA_BLOCK = 256
MOBA_TOPK = 3
CONV_W = 31
CONV_HALO = 32
N_NORM_GROUPS = 4
NEG_INF = float("-inf")
SB_DEAD = -150.0
V7X_LANES = 128
V7X_SUBLANES = 8
VMEM_LIMIT_BYTES = 56 * 1024 * 1024

COL_QA, COL_KA, COL_VA, COL_QB, COL_KB, COL_VB, COL_GLU_A, COL_GLU_G, COL_QD, COL_KD, COL_VD = range(11)

_NT = (((1,), (1,)), ((), ()))


def _dot(a, b):
    return jnp.dot(a, b, preferred_element_type=F32)


def _dot_nt(a, b):
    return lax.dot_general(a, b, _NT, preferred_element_type=F32)


def _rms(x, g):
    return x * lax.rsqrt(jnp.mean(x * x, axis=-1, keepdims=True) + EPS) * g


def _shift_div(x, n):
    assert n & (n - 1) == 0
    return lax.shift_right_logical(x, n.bit_length() - 1)


def _pick(n, candidates):
    for c in candidates:
        if n % c == 0:
            return c
    raise ValueError(f"no tile for {n} in {candidates}")


_SMALL_ROW_TILES = (512, 256, 128, 64, 32, 16, 8)


def _params(*sem):
    return pltpu.CompilerParams(dimension_semantics=sem, vmem_limit_bytes=VMEM_LIMIT_BYTES)


def _norm_matmul_kernel(x_ref, g_ref, w_ref, o_ref, h_ref):
    @pl.when(pl.program_id(1) == 0)
    def _():
        h_ref[...] = _rms(x_ref[...], g_ref[...]).astype(BF16)

    o_ref[...] = _dot(h_ref[...], w_ref[...])


def _norm_matmul(x, g, w):
    t, d = x.shape
    n = w.shape[1]
    tm = _pick(t, (1024,) + _SMALL_ROW_TILES)
    tn = _pick(n, (1408, 512, 256, 128))
    return pl.pallas_call(
        _norm_matmul_kernel,
        out_shape=jax.ShapeDtypeStruct((t, n), F32),
        grid=(t // tm, n // tn),
        in_specs=[pl.BlockSpec((tm, d), lambda i, j: (i, 0)),
                  pl.BlockSpec((1, d), lambda i, j: (0, 0)),
                  pl.BlockSpec((d, tn), lambda i, j: (0, j))],
        out_specs=pl.BlockSpec((tm, tn), lambda i, j: (i, j)),
        scratch_shapes=[pltpu.VMEM((tm, d), BF16)],
        compiler_params=_params("parallel", "arbitrary"),
        name="norm_in_proj",
    )(x, g, w)


def _out_proj_kernel(x_ref, a_ref, b_ref, c_ref, d_ref, w_ref, o_ref):
    acc = x_ref[...]
    for gi, m_ref in enumerate((a_ref, b_ref, c_ref, d_ref)):
        acc = acc + _dot(m_ref[...].astype(BF16), w_ref[gi * GROUP_W:(gi + 1) * GROUP_W, :])
    o_ref[...] = acc


def _out_proj(x, mixes, w):
    t, d = x.shape
    tm = _pick(t, (1024,) + _SMALL_ROW_TILES)
    mix_spec = pl.BlockSpec((tm, GROUP_W), lambda i: (i, 0))
    return pl.pallas_call(
        _out_proj_kernel,
        out_shape=jax.ShapeDtypeStruct((t, d), F32),
        grid=(t // tm,),
        in_specs=[pl.BlockSpec((tm, d), lambda i: (i, 0)), mix_spec, mix_spec, mix_spec, mix_spec,
                  pl.BlockSpec(w.shape, lambda i: (0, 0))],
        out_specs=pl.BlockSpec((tm, d), lambda i: (i, 0)),
        compiler_params=_params("parallel"),
        name="out_proj",
    )(x, *mixes, w)


def _ffn_kernel(x_ref, g_ref, wg_ref, wu_ref, wd_ref, gf_ref, o_ref, h_ref, acc_ref, *, final_norm):
    j = pl.program_id(1)

    @pl.when(j == 0)
    def _():
        h_ref[...] = _rms(x_ref[...], g_ref[...]).astype(BF16)
        acc_ref[...] = jnp.zeros_like(acc_ref)

    h = h_ref[...]
    gate = _dot(h, wg_ref[...])
    up = _dot(h, wu_ref[...])
    act = (gate * jax.nn.sigmoid(gate) * up).astype(BF16)
    acc_ref[...] += _dot(act, wd_ref[...])

    @pl.when(j == pl.num_programs(1) - 1)
    def _():
        y = x_ref[...] + acc_ref[...]
        if final_norm:
            y = _rms(y, gf_ref[...])
        o_ref[...] = y


def _ffn(x, g, wg, wu, wd, g_final, final_norm):
    t, d = x.shape
    f = wg.shape[1]
    tm = _pick(t, _SMALL_ROW_TILES)
    tf = _pick(f, (512, 256, 128))
    return pl.pallas_call(
        functools.partial(_ffn_kernel, final_norm=final_norm),
        out_shape=jax.ShapeDtypeStruct((t, d), F32),
        grid=(t // tm, f // tf),
        in_specs=[pl.BlockSpec((tm, d), lambda i, j: (i, 0)),
                  pl.BlockSpec((1, d), lambda i, j: (0, 0)),
                  pl.BlockSpec((d, tf), lambda i, j: (0, j)),
                  pl.BlockSpec((d, tf), lambda i, j: (0, j)),
                  pl.BlockSpec((tf, d), lambda i, j: (j, 0)),
                  pl.BlockSpec((1, d), lambda i, j: (0, 0))],
        out_specs=pl.BlockSpec((tm, d), lambda i, j: (i, 0)),
        scratch_shapes=[pltpu.VMEM((tm, d), BF16), pltpu.VMEM((tm, d), F32)],
        compiler_params=_params("parallel", "arbitrary"),
        name="ffn",
    )(x, g, wg, wu, wd, g_final)


def _lambda(lam_ref, lam_init):
    lp = lam_ref[...]
    s01 = jnp.sum(lp[0:1, :] * lp[1:2, :], axis=1, keepdims=True)
    s23 = jnp.sum(lp[2:3, :] * lp[3:4, :], axis=1, keepdims=True)
    return jnp.exp(s01) - jnp.exp(s23) + lam_init


def _softmax_step(s, v, m_ref, l_ref, acc_ref):
    m_prev = m_ref[...]
    m_new = jnp.maximum(m_prev, jnp.max(s, axis=1, keepdims=True))
    alpha = jnp.exp(m_prev - m_new)
    p = jnp.exp(s - m_new)
    l_ref[...] = alpha * l_ref[...] + jnp.sum(p, axis=1, keepdims=True)
    acc_ref[...] = alpha * acc_ref[...] + _dot(p.astype(BF16), v)
    m_ref[...] = m_new


def _softplus(z):
    return jnp.maximum(z, 0.0) + jnp.log(1.0 + jnp.exp(-jnp.abs(z)))


def _later_sum(lk, upper):
    hi = lk.astype(BF16)
    lo = (lk - hi.astype(F32)).astype(BF16)
    return _dot(hi, upper) + _dot(lo, upper)


def _suffix_matrix(n):
    j = lax.broadcasted_iota(jnp.int32, (n, n), 0)
    s = lax.broadcasted_iota(jnp.int32, (n, n), 1)
    return (j > s).astype(BF16)


def _attn_a_kernel(lam_ref, gsub_ref, q_ref, k_ref, v_ref, o_ref, m_ref, l_ref, acc_ref,
                   *, tq, tk, lam_init):
    qi = pl.program_id(1)
    q = q_ref[...] * (DH_A ** -0.5)
    lane = lax.broadcasted_iota(jnp.int32, (tq, DH), 1)
    qq = jnp.concatenate([jnp.where(lane < DH_A, q, 0.0), jnp.where(lane >= DH_A, q, 0.0)],
                         axis=0).astype(BF16)
    m_ref[...] = jnp.full_like(m_ref, NEG_INF)
    l_ref[...] = jnp.zeros_like(l_ref)
    acc_ref[...] = jnp.zeros_like(acc_ref)

    def body(kb, carry):
        start = pl.multiple_of(kb * tk, tk)
        k = k_ref[pl.ds(start, tk), :].astype(BF16)
        v = v_ref[pl.ds(start, tk), :].astype(BF16)
        s = _dot_nt(qq, k)
        row = lax.broadcasted_iota(jnp.int32, (2 * tq, tk), 0)
        col = lax.broadcasted_iota(jnp.int32, (2 * tq, tk), 1)
        q_pos = qi * tq + jnp.where(row >= tq, row - tq, row)
        s = jnp.where(start + col <= q_pos, s, NEG_INF)
        _softmax_step(s, v, m_ref, l_ref, acc_ref)
        return carry

    n_kb = (qi * tq + tq - 1) // tk + 1
    lax.fori_loop(0, n_kb, body, 0)

    acc = acc_ref[...]
    l = l_ref[...]
    lam = _lambda(lam_ref, lam_init)
    o = acc[:tq] / l[:tq] - lam * (acc[tq:] / l[tq:])
    o_ref[...] = (_rms(o, gsub_ref[...]) * (1.0 - lam_init)).astype(o_ref.dtype)


def _attn_a(proj, s, lam_a, g_sub, lam_init):
    tq = _pick(s, (256, 128))
    tk = _pick(s, (512, 256, 128))
    return pl.pallas_call(
        functools.partial(_attn_a_kernel, tq=tq, tk=tk, lam_init=lam_init),
        out_shape=jax.ShapeDtypeStruct((s, GROUP_W), BF16),
        grid=(N_HEADS, s // tq),
        in_specs=[pl.BlockSpec(lam_a.shape, lambda h, i: (0, 0)),
                  pl.BlockSpec((1, DH), lambda h, i: (0, 0)),
                  pl.BlockSpec((tq, DH), lambda h, i: (i, COL_QA * N_HEADS + h)),
                  pl.BlockSpec((s, DH), lambda h, i: (0, COL_KA * N_HEADS + h)),
                  pl.BlockSpec((s, DH), lambda h, i: (0, COL_VA * N_HEADS + h))],
        out_specs=pl.BlockSpec((tq, DH), lambda h, i: (i, h)),
        scratch_shapes=[pltpu.VMEM((2 * tq, 1), F32), pltpu.VMEM((2 * tq, 1), F32),
                        pltpu.VMEM((2 * tq, DH), F32)],
        compiler_params=_params("parallel", "arbitrary"),
        name="prompt_diff_attn",
    )(lam_a, g_sub, proj, proj, proj)


def _moba_select(gates, n_valid):
    lane = lax.broadcasted_iota(jnp.int32, gates.shape, 1)
    g = jnp.where(lane < n_valid, gates, NEG_INF)
    sel = jnp.zeros(gates.shape, F32)
    for _ in range(MOBA_TOPK):
        mx = jnp.max(g, axis=1, keepdims=True)
        pick = jnp.min(jnp.where(g == mx, lane, gates.shape[1]), axis=1, keepdims=True)
        hit = jnp.logical_and(lane == pick, mx > NEG_INF)
        sel = jnp.where(hit, 1.0, sel)
        g = jnp.where(lane == pick, NEG_INF, g)
    return sel


def _attn_b_kernel(q_ref, k_ref, v_ref, o_ref, kmean_ref, m_ref, l_ref, acc_ref, *, n_blocks):
    tq = MOBA_BLOCK
    qi = pl.program_id(1)

    @pl.when(qi == 0)
    def _():
        kmean_ref[...] = jnp.zeros_like(kmean_ref)

        def mean_body(b, carry):
            rows = k_ref[pl.ds(pl.multiple_of(b * MOBA_BLOCK, MOBA_BLOCK), MOBA_BLOCK), :]
            kmean_ref[pl.ds(b, 1), :] = jnp.mean(rows, axis=0, keepdims=True)
            return carry

        lax.fori_loop(0, n_blocks, mean_body, 0)

    q_raw = q_ref[...]
    gates = lax.dot_general(q_raw, kmean_ref[...], _NT, precision=lax.Precision.HIGHEST,
                            preferred_element_type=F32)
    sel = _moba_select(gates, qi).astype(BF16)
    q = (q_raw * (DH ** -0.5)).astype(BF16)

    m_ref[...] = jnp.full_like(m_ref, NEG_INF)
    l_ref[...] = jnp.zeros_like(l_ref)
    acc_ref[...] = jnp.zeros_like(acc_ref)

    def block(b, keep_fn):
        start = pl.multiple_of(b * MOBA_BLOCK, MOBA_BLOCK)
        k = k_ref[pl.ds(start, MOBA_BLOCK), :].astype(BF16)
        v = v_ref[pl.ds(start, MOBA_BLOCK), :].astype(BF16)
        s = _dot_nt(q, k)
        s = jnp.where(keep_fn(), s, NEG_INF)
        _softmax_step(s, v, m_ref, l_ref, acc_ref)

    def causal():
        row = lax.broadcasted_iota(jnp.int32, (tq, MOBA_BLOCK), 0)
        col = lax.broadcasted_iota(jnp.int32, (tq, MOBA_BLOCK), 1)
        return col <= row

    block(qi, causal)

    def body(b, carry):
        def chosen():
            onehot = (lax.broadcasted_iota(jnp.int32, (sel.shape[1], MOBA_BLOCK), 0) == b)
            return _dot(sel, onehot.astype(BF16)) > 0.5
        block(b, chosen)
        return carry

    lax.fori_loop(0, qi, body, 0)
    o_ref[...] = (acc_ref[...] / l_ref[...]).astype(o_ref.dtype)


def _attn_b(proj, s):
    assert s % MOBA_BLOCK == 0
    n_blocks = s // MOBA_BLOCK
    n_pad = -(-n_blocks // V7X_LANES) * V7X_LANES
    tq = MOBA_BLOCK
    return pl.pallas_call(
        functools.partial(_attn_b_kernel, n_blocks=n_blocks),
        out_shape=jax.ShapeDtypeStruct((s, GROUP_W), BF16),
        grid=(N_HEADS, s // tq),
        in_specs=[pl.BlockSpec((tq, DH), lambda h, i: (i, COL_QB * N_HEADS + h)),
                  pl.BlockSpec((s, DH), lambda h, i: (0, COL_KB * N_HEADS + h)),
                  pl.BlockSpec((s, DH), lambda h, i: (0, COL_VB * N_HEADS + h))],
        out_specs=pl.BlockSpec((tq, DH), lambda h, i: (i, h)),
        scratch_shapes=[pltpu.VMEM((n_pad, DH), F32), pltpu.VMEM((tq, 1), F32),
                        pltpu.VMEM((tq, 1), F32), pltpu.VMEM((tq, DH), F32)],
        compiler_params=_params("arbitrary", "arbitrary"),
        name="prompt_moba",
    )(proj, proj, proj)


def _attn_d_kernel(q_ref, k_ref, v_ref, o_ref, r_ref, acc_ref, *, tq, tk):
    qi = pl.program_id(1)
    q = (q_ref[...] * (DH ** -0.5)).astype(BF16)
    r_ref[...] = jnp.zeros_like(r_ref)
    acc_ref[...] = jnp.zeros_like(acc_ref)

    def cond(carry):
        kb, r_max = carry
        return jnp.logical_and(kb >= 0, r_max > SB_DEAD)

    def body(carry):
        kb, _ = carry
        start = pl.multiple_of(kb * tk, tk)
        k = k_ref[pl.ds(start, tk), :].astype(BF16)
        v = v_ref[pl.ds(start, tk), :].astype(BF16)
        z = _dot_nt(q, k)
        row = lax.broadcasted_iota(jnp.int32, (tq, tk), 0)
        col = lax.broadcasted_iota(jnp.int32, (tq, tk), 1)
        strict = start + col < qi * tq + row
        sp = _softplus(z)
        lk = jnp.where(strict, -sp, 0.0)
        later = _later_sum(lk, _suffix_matrix(tk))
        r = r_ref[...]
        w = jnp.where(strict, jnp.exp(z - sp + later + r), 0.0)
        acc_ref[...] += _dot(w.astype(BF16), v)
        r_new = r + jnp.sum(lk, axis=1, keepdims=True)
        r_ref[...] = r_new
        return kb - 1, jnp.max(r_new)

    lax.while_loop(cond, body, ((qi * tq + tq - 1) // tk, jnp.float32(0.0)))
    o_ref[...] = acc_ref[...].astype(o_ref.dtype)


def _attn_d(proj, s):
    tq = _pick(s, (256, 128))
    tk = _pick(s, (256, 128))
    return pl.pallas_call(
        functools.partial(_attn_d_kernel, tq=tq, tk=tk),
        out_shape=jax.ShapeDtypeStruct((s, GROUP_W), BF16),
        grid=(N_HEADS, s // tq),
        in_specs=[pl.BlockSpec((tq, DH), lambda h, i: (i, COL_QD * N_HEADS + h)),
                  pl.BlockSpec((s, DH), lambda h, i: (0, COL_KD * N_HEADS + h)),
                  pl.BlockSpec((s, DH), lambda h, i: (0, COL_VD * N_HEADS + h))],
        out_specs=pl.BlockSpec((tq, DH), lambda h, i: (i, h)),
        scratch_shapes=[pltpu.VMEM((tq, 1), F32), pltpu.VMEM((tq, DH), F32)],
        compiler_params=_params("parallel", "arbitrary"),
        name="prompt_stickbreak",
    )(proj, proj, proj)


def _conv_kernel(*refs, tt, has_prev):
    if has_prev:
        cur_ref, prev_ref, cbuf_ref, bglu_ref, wdw_ref, bdw_ref, gn_ref, bn_ref, o_ref, nbuf_ref, ubuf_ref = refs
    else:
        cur_ref, cbuf_ref, bglu_ref, wdw_ref, bdw_ref, gn_ref, bn_ref, o_ref, nbuf_ref, ubuf_ref = refs
    i = pl.program_id(1)
    c = GROUP_W

    def glu(x):
        x = x + bglu_ref[...]
        return x[:, :c] * jax.nn.sigmoid(x[:, c:])

    ubuf_ref[CONV_HALO:CONV_HALO + tt, :] = glu(cur_ref[...])

    @pl.when(i == 0)
    def _():
        ubuf_ref[CONV_HALO - (CONV_W - 1):CONV_HALO, :] = cbuf_ref[...]

    if has_prev:
        @pl.when(i > 0)
        def _():
            ubuf_ref[0:CONV_HALO, :] = glu(prev_ref[...])

    off = CONV_HALO - (CONV_W - 1)
    y = jnp.zeros((tt, c), F32) + bdw_ref[...]
    for j in range(CONV_W):
        y = y + ubuf_ref[off + j:off + j + tt, :] * wdw_ref[j:j + 1, :]

    gw = c // N_NORM_GROUPS
    for gi in range(N_NORM_GROUPS):
        seg = y[:, gi * gw:(gi + 1) * gw]
        mu = jnp.mean(seg, axis=-1, keepdims=True)
        var = jnp.mean(jnp.square(seg - mu), axis=-1, keepdims=True)
        yn = (seg - mu) * lax.rsqrt(var + EPS)
        yn = yn * gn_ref[:, gi * gw:(gi + 1) * gw] + bn_ref[:, gi * gw:(gi + 1) * gw]
        o_ref[:, gi * gw:(gi + 1) * gw] = (yn * jax.nn.sigmoid(yn)).astype(o_ref.dtype)

    @pl.when(i == pl.num_programs(1) - 1)
    def _():
        nbuf_ref[...] = ubuf_ref[CONV_HALO + tt - (CONV_W - 1):CONV_HALO + tt, :]


def _conv_group(proj, batch, s, conv_buf, b_glu, w_dw, b_dw, g_norm, b_norm, out_dtype):
    tt = _pick(s, (512, 256, 128, 64, 32, 16, 8))
    nt = s // tt
    has_prev = nt > 1
    assert tt + CONV_HALO >= CONV_W - 1 and (not has_prev or tt % CONV_HALO == 0)
    c = GROUP_W
    glu_col = COL_GLU_A // 2
    in_specs = [pl.BlockSpec((tt, 2 * c), lambda b, i: (b * nt + i, glu_col))]
    args = [proj]
    if has_prev:
        per = tt // CONV_HALO
        in_specs.append(pl.BlockSpec((CONV_HALO, 2 * c),
                                     lambda b, i: (jnp.maximum((b * nt + i) * per - 1, 0), glu_col)))
        args.append(proj)
    vec = lambda n: pl.BlockSpec((1, n), lambda b, i: (0, 0))
    in_specs += [pl.BlockSpec((None, CONV_W - 1, c), lambda b, i: (b, 0, 0)),
                 vec(2 * c), pl.BlockSpec((CONV_W, c), lambda b, i: (0, 0)), vec(c), vec(c), vec(c)]
    args += [conv_buf, b_glu, w_dw, b_dw, g_norm, b_norm]
    return pl.pallas_call(
        functools.partial(_conv_kernel, tt=tt, has_prev=has_prev),
        out_shape=(jax.ShapeDtypeStruct((batch * s, c), out_dtype),
                   jax.ShapeDtypeStruct((batch, CONV_W - 1, c), F32)),
        grid=(batch, nt),
        in_specs=in_specs,
        out_specs=(pl.BlockSpec((tt, c), lambda b, i: (b * nt + i, 0)),
                   pl.BlockSpec((None, CONV_W - 1, c), lambda b, i: (b, 0, 0))),
        scratch_shapes=[pltpu.VMEM((CONV_HALO + tt, c), F32)],
        compiler_params=_params("parallel", "arbitrary"),
        name="conv_module",
    )(*args)


def _block_diag_queries(q, n_maps):
    nq = q.shape[0]
    reps = n_maps * N_HEADS
    qt = jnp.concatenate([q] * reps, axis=0)
    row = lax.broadcasted_iota(jnp.int32, qt.shape, 0)
    lane = lax.broadcasted_iota(jnp.int32, qt.shape, 1)
    width = DH // n_maps
    return jnp.where(_shift_div(row, nq) == _shift_div(lane, width), qt, 0.0)


def _pad_rows(dst_ref, src_ref):
    dst_ref[...] = jnp.zeros_like(dst_ref)
    dst_ref[0:src_ref.shape[0], :] = src_ref[...]


def _dec_a_kernel(pt_ref, lam_ref, gsub_ref, q_ref, kn_ref, vn_ref, *rest, n_pg, nq, lam_init):
    del pt_ref
    k_pages, v_pages = rest[:n_pg], rest[n_pg:2 * n_pg]
    o_ref, qbd_ref, kpad_ref, vpad_ref, m_ref, l_ref, acc_ref = rest[2 * n_pg:]
    g = pl.program_id(1)
    rows = 2 * N_HEADS * nq

    @pl.when(g == 0)
    def _():
        qbd_ref[...] = _block_diag_queries(q_ref[...] * (DH_A ** -0.5), 2).astype(BF16)
        _pad_rows(kpad_ref, kn_ref)
        _pad_rows(vpad_ref, vn_ref)
        s = _dot_nt(qbd_ref[...], kpad_ref[...].astype(BF16))
        row = lax.broadcasted_iota(jnp.int32, s.shape, 0)
        col = lax.broadcasted_iota(jnp.int32, s.shape, 1)
        s = jnp.where(col <= (row & (nq - 1)), s, NEG_INF)
        m_ref[...] = jnp.full_like(m_ref, NEG_INF)
        l_ref[...] = jnp.zeros_like(l_ref)
        acc_ref[...] = jnp.zeros_like(acc_ref)
        _softmax_step(s, vpad_ref[...].astype(BF16), m_ref, l_ref, acc_ref)

    qbd = qbd_ref[...]
    for p in range(n_pg):
        s = _dot_nt(qbd, k_pages[p][...].astype(BF16))
        _softmax_step(s, v_pages[p][...].astype(BF16), m_ref, l_ref, acc_ref)

    @pl.when(g == pl.num_programs(1) - 1)
    def _():
        acc = acc_ref[...]
        l = l_ref[...]
        lam = _lambda(lam_ref, lam_init)
        assert acc.shape[0] == rows
        for h in range(N_HEADS):
            r1, r2, c0 = 2 * h * nq, (2 * h + 1) * nq, h * DH
            o1 = acc[r1:r1 + nq, c0:c0 + DH] / l[r1:r1 + nq]
            o2 = acc[r2:r2 + nq, c0:c0 + DH] / l[r2:r2 + nq]
            o = o1 - lam * o2
            o_ref[:, c0:c0 + DH] = _rms(o, gsub_ref[...]) * (1.0 - lam_init)


def _dec_d_kernel(pt_ref, q_ref, kn_ref, vn_ref, *rest, n_pg, nq, page):
    del pt_ref
    k_pages, v_pages = rest[:n_pg], rest[n_pg:2 * n_pg]
    o_ref, qbd_ref, kpad_ref, vpad_ref, r_ref, acc_ref, rmax_ref = rest[2 * n_pg:]
    g = pl.program_id(1)

    def step(z, v, strict, r):
        sp = _softplus(z)
        lk = -sp if strict is None else jnp.where(strict, -sp, 0.0)
        later = _later_sum(lk, _suffix_matrix(page))
        w = jnp.exp(z - sp + later + r)
        if strict is not None:
            w = jnp.where(strict, w, 0.0)
        acc_ref[...] += _dot(w.astype(BF16), v)
        r_new = r + jnp.sum(lk, axis=1, keepdims=True)
        r_ref[...] = r_new
        rmax_ref[0] = jnp.max(r_new)

    @pl.when(g == 0)
    def _():
        qbd_ref[...] = _block_diag_queries(q_ref[...] * (DH ** -0.5), 1).astype(BF16)
        _pad_rows(kpad_ref, kn_ref)
        _pad_rows(vpad_ref, vn_ref)
        acc_ref[...] = jnp.zeros_like(acc_ref)
        z = _dot_nt(qbd_ref[...], kpad_ref[...].astype(BF16))
        row = lax.broadcasted_iota(jnp.int32, z.shape, 0)
        col = lax.broadcasted_iota(jnp.int32, z.shape, 1)
        step(z, vpad_ref[...].astype(BF16), col < (row & (nq - 1)), jnp.zeros(r_ref.shape, F32))

    for p in range(n_pg):
        @pl.when(rmax_ref[0] > SB_DEAD)
        def _():
            z = _dot_nt(qbd_ref[...], k_pages[p][...].astype(BF16))
            step(z, v_pages[p][...].astype(BF16), None, r_ref[...])

    @pl.when(g == pl.num_programs(1) - 1)
    def _():
        acc = acc_ref[...]
        for h in range(N_HEADS):
            o_ref[:, h * DH:(h + 1) * DH] = acc[h * nq:(h + 1) * nq, h * DH:(h + 1) * DH]


def _dec_b_kernel(pt_ref, q_ref, kn_ref, vn_ref, *rest, n_pg, nq, page):
    del pt_ref
    k_pages, v_pages = rest[:n_pg], rest[n_pg:2 * n_pg]
    (o_ref, qf_ref, qbd_ref, kpad_ref, vpad_ref,
     sg_ref, si_ref, sm_ref, sl_ref, sacc_ref) = rest[2 * n_pg:]
    g = pl.program_id(1)
    pages_per_block = MOBA_BLOCK // page
    scale = DH ** -0.5

    @pl.when(g == 0)
    def _():
        qf = _block_diag_queries(q_ref[...], 1)
        qf_ref[...] = qf
        qbd_ref[...] = (qf * scale).astype(BF16)
        sg_ref[...] = jnp.full_like(sg_ref, NEG_INF)
        for slot in range(MOBA_TOPK):
            si_ref[slot] = jnp.full(si_ref.shape[1:], -1.0 - slot, F32)
        sm_ref[...] = jnp.full_like(sm_ref, NEG_INF)
        sl_ref[...] = jnp.zeros_like(sl_ref)
        sacc_ref[...] = jnp.zeros_like(sacc_ref)

    qbd = qbd_ref[...]
    for blk in range(n_pg // pages_per_block):
        ks = [k_pages[blk * pages_per_block + t][...] for t in range(pages_per_block)]
        vs = [v_pages[blk * pages_per_block + t][...].astype(BF16) for t in range(pages_per_block)]
        k_mean = sum(jnp.sum(kk, axis=0, keepdims=True) for kk in ks) * (1.0 / MOBA_BLOCK)
        gate = jnp.sum(qf_ref[...] * k_mean, axis=1, keepdims=True)
        s = jnp.concatenate([_dot_nt(qbd, kk.astype(BF16)) for kk in ks], axis=1)
        m_b = jnp.max(s, axis=1, keepdims=True)
        p = jnp.exp(s - m_b)
        l_b = jnp.sum(p, axis=1, keepdims=True)
        acc_b = sum(_dot(p[:, t * page:(t + 1) * page].astype(BF16), vs[t])
                    for t in range(pages_per_block))
        b_idx = (g * (n_pg // pages_per_block) + blk).astype(F32)

        gs = [sg_ref[t] for t in range(MOBA_TOPK)]
        ids = [si_ref[t] for t in range(MOBA_TOPK)]

        def worse(a, b):
            return jnp.logical_or(gs[a] < gs[b], jnp.logical_and(gs[a] == gs[b], ids[a] > ids[b]))

        w0 = jnp.logical_and(worse(0, 1), worse(0, 2))
        w1 = jnp.logical_and(worse(1, 0), worse(1, 2))
        w2 = jnp.logical_not(jnp.logical_or(w0, w1))
        g_worst = jnp.where(w0, gs[0], jnp.where(w1, gs[1], gs[2]))
        replace = gate > g_worst
        for slot, is_worst in enumerate((w0, w1, w2)):
            take = jnp.logical_and(replace, is_worst)
            sg_ref[slot] = jnp.where(take, gate, gs[slot])
            si_ref[slot] = jnp.where(take, b_idx, ids[slot])
            sm_ref[slot] = jnp.where(take, m_b, sm_ref[slot])
            sl_ref[slot] = jnp.where(take, l_b, sl_ref[slot])
            sacc_ref[slot] = jnp.where(take, acc_b, sacc_ref[slot])

    @pl.when(g == pl.num_programs(1) - 1)
    def _():
        _pad_rows(kpad_ref, kn_ref)
        _pad_rows(vpad_ref, vn_ref)
        s = _dot_nt(qbd, kpad_ref[...].astype(BF16))
        row = lax.broadcasted_iota(jnp.int32, s.shape, 0)
        col = lax.broadcasted_iota(jnp.int32, s.shape, 1)
        s = jnp.where(col <= (row & (nq - 1)), s, NEG_INF)
        m_o = jnp.max(s, axis=1, keepdims=True)
        p = jnp.exp(s - m_o)
        m_all = m_o
        for slot in range(MOBA_TOPK):
            m_all = jnp.maximum(m_all, sm_ref[slot])
        w_o = jnp.exp(m_o - m_all)
        num = w_o * _dot(p.astype(BF16), vpad_ref[...].astype(BF16))
        den = w_o * jnp.sum(p, axis=1, keepdims=True)
        for slot in range(MOBA_TOPK):
            w_s = jnp.exp(sm_ref[slot] - m_all)
            num = num + w_s * sacc_ref[slot]
            den = den + w_s * sl_ref[slot]
        out = num / den
        for h in range(N_HEADS):
            o_ref[:, h * DH:(h + 1) * DH] = out[h * nq:(h + 1) * nq, h * DH:(h + 1) * DH]


def _dec_attention(kind, proj, k_cache, v_cache, page_table, li, nq, cols, extra=(), **kw):
    n_seq, n_pages = page_table.shape
    page = k_cache.shape[2]
    n_pg = _pick(n_pages, (8, 4, 2))
    steps = n_pages // n_pg
    col_q, col_k, col_v = cols
    w = GROUP_W
    assert page == V7X_LANES and nq % V7X_SUBLANES == 0 and nq <= page
    assert (n_pages * page) % MOBA_BLOCK == 0 and MOBA_BLOCK % page == 0

    descending = kind == "d"

    def page_spec(t):
        def imap(b, g, pt):
            idx = g * n_pg + t
            if descending:
                idx = n_pages - 1 - idx
            return (li, pt[b, idx], 0, 0)
        return pl.BlockSpec((None, None, page, w), imap)

    small = lambda a: pl.BlockSpec(a.shape, lambda b, g, pt: (0,) * a.ndim)
    row_spec = lambda col: pl.BlockSpec((nq, w), lambda b, g, pt: (b, col))
    in_specs = ([small(a) for a in extra] + [row_spec(col_q), row_spec(col_k), row_spec(col_v)]
                + [page_spec(t) for t in range(n_pg)] * 2)
    args = list(extra) + [proj, proj, proj] + [k_cache] * n_pg + [v_cache] * n_pg

    pad = [pltpu.VMEM((page, w), F32), pltpu.VMEM((page, w), F32)]
    if kind == "a":
        rows = 2 * N_HEADS * nq
        body = functools.partial(_dec_a_kernel, n_pg=n_pg, nq=nq, **kw)
        scratch = [pltpu.VMEM((rows, w), BF16)] + pad + [
            pltpu.VMEM((rows, 1), F32), pltpu.VMEM((rows, 1), F32), pltpu.VMEM((rows, w), F32)]
    elif kind == "d":
        rows = N_HEADS * nq
        body = functools.partial(_dec_d_kernel, n_pg=n_pg, nq=nq, page=page)
        scratch = [pltpu.VMEM((rows, w), BF16)] + pad + [
            pltpu.VMEM((rows, 1), F32), pltpu.VMEM((rows, w), F32), pltpu.SMEM((1,), F32)]
    else:
        rows = N_HEADS * nq
        body = functools.partial(_dec_b_kernel, n_pg=n_pg, nq=nq, page=page)
        scratch = [pltpu.VMEM((rows, w), F32), pltpu.VMEM((rows, w), BF16)] + pad + [
            pltpu.VMEM((MOBA_TOPK, rows, 1), F32), pltpu.VMEM((MOBA_TOPK, rows, 1), F32),
            pltpu.VMEM((MOBA_TOPK, rows, 1), F32), pltpu.VMEM((MOBA_TOPK, rows, 1), F32),
            pltpu.VMEM((MOBA_TOPK, rows, w), F32)]

    return pl.pallas_call(
        body,
        out_shape=jax.ShapeDtypeStruct((n_seq * nq, w), F32),
        grid_spec=pltpu.PrefetchScalarGridSpec(
            num_scalar_prefetch=1,
            grid=(n_seq, steps),
            in_specs=in_specs,
            out_specs=pl.BlockSpec((nq, w), lambda b, g, pt: (b, 0)),
            scratch_shapes=scratch),
        compiler_params=_params("parallel", "arbitrary"),
        name="sample_attn_" + kind,
    )(page_table, *args)


def _split_heads(proj, col, lead):
    return proj[:, col * GROUP_W:(col + 1) * GROUP_W].reshape(lead + (N_HEADS, DH))


def kernel(x_prompt, x_sample, cache_a_k, cache_a_v, cache_b_k, cache_b_v, cache_d_k, cache_d_v, state_c_conv, page_table, g_mix, w_in, lam_a, g_a_sub, b_c_glu, w_c_dw, b_c_dw, g_c_norm, b_c_norm, w_out, g_ffn, w_gate, w_up, w_down, g_final):
    n_b, s, d = x_prompt.shape
    n_seq, nq, _ = x_sample.shape
    depth = w_in.shape[0]
    assert n_b == 1, "prompt kernels keep one sequence's keys resident"

    def paged(c):
        return c.reshape(c.shape[:3] + (c.shape[3] * c.shape[4],))

    caches = [paged(c) for c in (cache_a_k, cache_a_v, cache_b_k, cache_b_v, cache_d_k, cache_d_v)]
    row = lambda v: v.reshape(1, -1)
    zero_buf = jnp.zeros((n_b, CONV_W - 1, GROUP_W), F32)

    xp = x_prompt.reshape(n_b * s, d)
    xs = x_sample.reshape(n_seq * nq, d)
    new_p = [[] for _ in range(7)]
    new_s = [[] for _ in range(7)]
    kv_cols = (COL_KA, COL_VA, COL_KB, COL_VB, COL_KD, COL_VD)

    for li in range(depth):
        w_in_l, w_out_l = w_in[li].astype(BF16), w_out[li].astype(BF16)
        wg_l, wu_l, wd_l = w_gate[li].astype(BF16), w_up[li].astype(BF16), w_down[li].astype(BF16)
        lam_init = 0.8 - 0.6 * math.exp(-0.3 * li)
        conv_w = (row(b_c_glu[li]), w_c_dw[li], row(b_c_dw[li]), row(g_c_norm[li]), row(b_c_norm[li]))
        last = li == depth - 1

        proj = _norm_matmul(xp, row(g_mix[li]), w_in_l)
        o_a = _attn_a(proj, s, lam_a[li], row(g_a_sub[li]), lam_init)
        o_b = _attn_b(proj, s)
        o_c, buf_p = _conv_group(proj, n_b, s, zero_buf, *conv_w, out_dtype=BF16)
        o_d = _attn_d(proj, s)
        xp = _out_proj(xp, (o_a, o_b, o_c, o_d), w_out_l)
        xp = _ffn(xp, row(g_ffn[li]), wg_l, wu_l, wd_l, row(g_final), final_norm=last)
        for acc, col in zip(new_p, kv_cols):
            acc.append(_split_heads(proj, col, (n_b, s)))
        new_p[6].append(buf_p)

        proj = _norm_matmul(xs, row(g_mix[li]), w_in_l)
        o_a = _dec_attention("a", proj, caches[0], caches[1], page_table, li, nq,
                             (COL_QA, COL_KA, COL_VA), extra=(lam_a[li], row(g_a_sub[li])),
                             lam_init=lam_init)
        o_b = _dec_attention("b", proj, caches[2], caches[3], page_table, li, nq,
                             (COL_QB, COL_KB, COL_VB))
        o_c, buf_s = _conv_group(proj, n_seq, nq, state_c_conv[li], *conv_w, out_dtype=F32)
        o_d = _dec_attention("d", proj, caches[4], caches[5], page_table, li, nq,
                             (COL_QD, COL_KD, COL_VD))
        xs = _out_proj(xs, (o_a, o_b, o_c, o_d), w_out_l)
        xs = _ffn(xs, row(g_ffn[li]), wg_l, wu_l, wd_l, row(g_final), final_norm=last)
        for acc, col in zip(new_s, kv_cols):
            acc.append(_split_heads(proj, col, (n_seq, nq)))
        new_s[6].append(buf_s)

    outs_p = [jnp.stack(v, axis=0) for v in new_p]
    outs_s = [jnp.stack(v, axis=0) for v in new_s]
    return (xp.reshape(n_b, s, d), xs.reshape(n_seq, nq, d), *outs_p, *outs_s)
```

```python
import functools
import math

import jax
import jax.numpy as jnp
from jax import lax
from jax.experimental import pallas as pl
from jax.experimental.pallas import tpu as pltpu

F32 = jnp.float32
BF16 = jnp.bfloat16

EPS = 1e-6
N_HEADS = 4
DH = 128
GROUP_W = N_HEADS * DH
DH_A = DH // 2
MOBA_BLOCK = 256
MOBA_TOPK = 3
CONV_W = 31
CONV_HALO = 32
N_NORM_GROUPS = 4
NEG_INF = float("-inf")
M_INIT = -1e30
MASK_BIAS = -2.0 ** 100
LOG2E = math.log2(math.e)
SB_DEAD = -150.0
V7X_LANES = 128
V7X_SUBLANES = 8
VMEM_LIMIT_BYTES = 56 * 1024 * 1024

COL_QA, COL_KA, COL_VA, COL_QB, COL_KB, COL_VB, COL_GLU_A, COL_GLU_G, COL_QD, COL_KD, COL_VD = range(11)

_NT = (((1,), (1,)), ((), ()))
_SMALL_ROW_TILES = (512, 256, 128, 64, 32, 16, 8)


def _dot(a, b):
    return jnp.dot(a, b, preferred_element_type=F32)


def _dot_nt(a, b):
    return lax.dot_general(a, b, _NT, preferred_element_type=F32)


def _rms(x, g):
    return x * lax.rsqrt(jnp.mean(x * x, axis=-1, keepdims=True) + EPS) * g


def _shift_div(x, n):
    assert n & (n - 1) == 0
    return lax.shift_right_logical(x, n.bit_length() - 1)


def _pick(n, candidates):
    for c in candidates:
        if n % c == 0:
            return c
    raise ValueError(f"no tile for {n} in {candidates}")


def _params(*sem):
    return pltpu.CompilerParams(dimension_semantics=sem, vmem_limit_bytes=VMEM_LIMIT_BYTES)


def _norm_matmul_kernel(x_ref, g_ref, w_ref, o_ref, *rest):
    h_ref = rest[-1]

    @pl.when(pl.program_id(1) == 0)
    def _():
        h_ref[...] = _rms(x_ref[...], g_ref[...]).astype(BF16)

    y = _dot(h_ref[...], w_ref[...])
    o_ref[...] = y
    if len(rest) == 2:
        rest[0][...] = y.astype(BF16)


def _norm_matmul(x, g, w, with_bf16):
    t, d = x.shape
    n = w.shape[1]
    tm = _pick(t, (1024,) + _SMALL_ROW_TILES)
    tn = _pick(n, (1408, 512, 256, 128))
    out_spec = pl.BlockSpec((tm, tn), lambda i, j: (i, j))
    out_shape = [jax.ShapeDtypeStruct((t, n), F32)]
    if with_bf16:
        out_shape.append(jax.ShapeDtypeStruct((t, n), BF16))
    return pl.pallas_call(
        _norm_matmul_kernel,
        out_shape=out_shape,
        grid=(t // tm, n // tn),
        in_specs=[pl.BlockSpec((tm, d), lambda i, j: (i, 0)),
                  pl.BlockSpec((1, d), lambda i, j: (0, 0)),
                  pl.BlockSpec((d, tn), lambda i, j: (0, j))],
        out_specs=[out_spec] * len(out_shape),
        scratch_shapes=[pltpu.VMEM((tm, d), BF16)],
        compiler_params=_params("parallel", "arbitrary"),
        name="norm_in_proj",
    )(x, g, w)


def _out_proj_kernel(x_ref, a_ref, b_ref, c_ref, d_ref, w_ref, o_ref):
    acc = x_ref[...]
    for gi, m_ref in enumerate((a_ref, b_ref, c_ref, d_ref)):
        acc = acc + _dot(m_ref[...].astype(BF16), w_ref[gi * GROUP_W:(gi + 1) * GROUP_W, :])
    o_ref[...] = acc


def _out_proj(x, mixes, w):
    t, d = x.shape
    tm = _pick(t, (1024,) + _SMALL_ROW_TILES)
    mix_spec = pl.BlockSpec((tm, GROUP_W), lambda i: (i, 0))
    return pl.pallas_call(
        _out_proj_kernel,
        out_shape=jax.ShapeDtypeStruct((t, d), F32),
        grid=(t // tm,),
        in_specs=[pl.BlockSpec((tm, d), lambda i: (i, 0)), mix_spec, mix_spec, mix_spec, mix_spec,
                  pl.BlockSpec(w.shape, lambda i: (0, 0))],
        out_specs=pl.BlockSpec((tm, d), lambda i: (i, 0)),
        compiler_params=_params("parallel"),
        name="out_proj",
    )(x, *mixes, w)


def _ffn_kernel(x_ref, g_ref, wg_ref, wu_ref, wd_ref, gf_ref, o_ref, h_ref, acc_ref, *, final_norm):
    j = pl.program_id(1)

    @pl.when(j == 0)
    def _():
        h_ref[...] = _rms(x_ref[...], g_ref[...]).astype(BF16)
        acc_ref[...] = jnp.zeros_like(acc_ref)

    h = h_ref[...]
    gate = _dot(h, wg_ref[...])
    up = _dot(h, wu_ref[...])
    act = (gate * jax.nn.sigmoid(gate) * up).astype(BF16)
    acc_ref[...] += _dot(act, wd_ref[...])

    @pl.when(j == pl.num_programs(1) - 1)
    def _():
        y = x_ref[...] + acc_ref[...]
        if final_norm:
            y = _rms(y, gf_ref[...])
        o_ref[...] = y


def _ffn(x, g, wg, wu, wd, g_final, final_norm):
    t, d = x.shape
    f = wg.shape[1]
    tm = _pick(t, _SMALL_ROW_TILES)
    tf = _pick(f, (512, 256, 128))
    return pl.pallas_call(
        functools.partial(_ffn_kernel, final_norm=final_norm),
        out_shape=jax.ShapeDtypeStruct((t, d), F32),
        grid=(t // tm, f // tf),
        in_specs=[pl.BlockSpec((tm, d), lambda i, j: (i, 0)),
                  pl.BlockSpec((1, d), lambda i, j: (0, 0)),
                  pl.BlockSpec((d, tf), lambda i, j: (0, j)),
                  pl.BlockSpec((d, tf), lambda i, j: (0, j)),
                  pl.BlockSpec((tf, d), lambda i, j: (j, 0)),
                  pl.BlockSpec((1, d), lambda i, j: (0, 0))],
        out_specs=pl.BlockSpec((tm, d), lambda i, j: (i, 0)),
        scratch_shapes=[pltpu.VMEM((tm, d), BF16), pltpu.VMEM((tm, d), F32)],
        compiler_params=_params("parallel", "arbitrary"),
        name="ffn",
    )(x, g, wg, wu, wd, g_final)


def _lambda(lam_ref, lam_init):
    lp = lam_ref[...]
    s01 = jnp.sum(lp[0:1, :] * lp[1:2, :], axis=1, keepdims=True)
    s23 = jnp.sum(lp[2:3, :] * lp[3:4, :], axis=1, keepdims=True)
    return jnp.exp(s01) - jnp.exp(s23) + lam_init


def _split_maps(q):
    lane = lax.broadcasted_iota(jnp.int32, q.shape, 1)
    return jnp.concatenate([jnp.where(lane < DH_A, q, 0.0), jnp.where(lane >= DH_A, q, 0.0)], axis=0)


def _softmax_update(s, m_ref, l_ref):
    m_prev = m_ref[...]
    m_new = jnp.maximum(m_prev, jnp.max(s, axis=1, keepdims=True))
    alpha = jnp.exp2(m_prev - m_new)
    p = jnp.exp2(s - m_new)
    l_ref[...] = alpha * l_ref[...] + jnp.sum(p, axis=1, keepdims=True)
    m_ref[...] = m_new
    return p, alpha


def _flash_loop(segments, n_total, scores_fn, values_fn, m_ref, l_ref, acc_ref, s_ref, p_ref, a_ref):
    m_ref[...] = jnp.full_like(m_ref, M_INIT)
    l_ref[...] = jnp.zeros_like(l_ref)
    acc_ref[...] = jnp.zeros_like(acc_ref)
    p_ref[...] = jnp.zeros_like(p_ref)
    a_ref[...] = jnp.ones_like(a_ref)
    s_ref[0] = scores_fn(0)

    def finish(j):
        acc_ref[...] = a_ref[...] * acc_ref[...] + _dot(p_ref[...], values_fn(j))

    def make_body(keep_fn):
        def body(j, carry):
            finish(jnp.maximum(j - 1, 0))
            slot = j & 1
            s = s_ref[slot]
            s_ref[1 - slot] = scores_fn(jnp.minimum(j + 1, n_total - 1))
            if keep_fn is not None:
                s = jnp.where(keep_fn(j), s, NEG_INF)
            p, alpha = _softmax_update(s, m_ref, l_ref)
            a_ref[...] = alpha
            p_ref[...] = p.astype(BF16)
            return carry
        return body

    for lo, hi, keep_fn in segments:
        lax.fori_loop(lo, hi, make_body(keep_fn), 0)
    finish(n_total - 1)


def _flash_scratch(rows, tk):
    return [pltpu.VMEM((rows, 1), F32), pltpu.VMEM((rows, 1), F32), pltpu.VMEM((rows, DH), F32),
            pltpu.VMEM((2, rows, tk), F32), pltpu.VMEM((rows, tk), BF16), pltpu.VMEM((rows, 1), F32)]


def _softplus(z):
    return jnp.maximum(z, 0.0) + jnp.log(1.0 + jnp.exp(-jnp.abs(z)))


def _later_sum(lk, upper):
    hi = lk.astype(BF16)
    lo = (lk - hi.astype(F32)).astype(BF16)
    return _dot(hi, upper) + _dot(lo, upper)


def _suffix_matrix(n):
    j = lax.broadcasted_iota(jnp.int32, (n, n), 0)
    s = lax.broadcasted_iota(jnp.int32, (n, n), 1)
    return (j > s).astype(BF16)


def _attn_a_kernel(lam_ref, gsub_ref, q_ref, k_ref, v_ref, o_ref, *scratch, tq, tk, lam_init):
    qi = pl.program_id(1)
    qq = _split_maps(q_ref[...] * (DH_A ** -0.5 * LOG2E)).astype(BF16)

    def block(ref, j):
        return ref[pl.ds(pl.multiple_of(j * tk, tk), tk), :]

    def causal(j):
        row = lax.broadcasted_iota(jnp.int32, (2 * tq, tk), 0)
        col = lax.broadcasted_iota(jnp.int32, (2 * tq, tk), 1)
        q_pos = qi * tq + jnp.where(row >= tq, row - tq, row)
        return j * tk + col <= q_pos

    n_full = (qi * tq) // tk
    n_total = (qi * tq + tq - 1) // tk + 1
    _flash_loop([(0, n_full, None), (n_full, n_total, causal)], n_total,
                lambda j: _dot_nt(qq, block(k_ref, j)), lambda j: block(v_ref, j), *scratch)

    m_ref, l_ref, acc_ref = scratch[:3]
    acc = acc_ref[...]
    l = l_ref[...]
    lam = _lambda(lam_ref, lam_init)
    o = acc[:tq] / l[:tq] - lam * (acc[tq:] / l[tq:])
    o_ref[...] = (_rms(o, gsub_ref[...]) * (1.0 - lam_init)).astype(o_ref.dtype)


def _attn_a(proj, proj_b, s, lam_a, g_sub, lam_init):
    tq = _pick(s, (256, 128))
    tk = _pick(s, (512, 256, 128))
    return pl.pallas_call(
        functools.partial(_attn_a_kernel, tq=tq, tk=tk, lam_init=lam_init),
        out_shape=jax.ShapeDtypeStruct((s, GROUP_W), BF16),
        grid=(N_HEADS, s // tq),
        in_specs=[pl.BlockSpec(lam_a.shape, lambda h, i: (0, 0)),
                  pl.BlockSpec((1, DH), lambda h, i: (0, 0)),
                  pl.BlockSpec((tq, DH), lambda h, i: (i, COL_QA * N_HEADS + h)),
                  pl.BlockSpec((s, DH), lambda h, i: (0, COL_KA * N_HEADS + h)),
                  pl.BlockSpec((s, DH), lambda h, i: (0, COL_VA * N_HEADS + h))],
        out_specs=pl.BlockSpec((tq, DH), lambda h, i: (i, h)),
        scratch_shapes=_flash_scratch(2 * tq, tk),
        compiler_params=_params("parallel", "arbitrary"),
        name="prompt_diff_attn",
    )(lam_a, g_sub, proj, proj_b, proj_b)


def _block_mean_kernel(k_ref, o_ref, *, n_blocks):
    o_ref[...] = jnp.zeros_like(o_ref)

    def body(b, carry):
        rows = k_ref[pl.ds(pl.multiple_of(b * MOBA_BLOCK, MOBA_BLOCK), MOBA_BLOCK), :]
        o_ref[pl.ds(b, 1), :] = jnp.mean(rows, axis=0, keepdims=True)
        return carry

    lax.fori_loop(0, n_blocks, body, 0)


def _block_means(proj, s, n_pad):
    return pl.pallas_call(
        functools.partial(_block_mean_kernel, n_blocks=s // MOBA_BLOCK),
        out_shape=jax.ShapeDtypeStruct((N_HEADS, n_pad, DH), F32),
        grid=(N_HEADS,),
        in_specs=[pl.BlockSpec((s, DH), lambda h: (0, COL_KB * N_HEADS + h))],
        out_specs=pl.BlockSpec((None, n_pad, DH), lambda h: (h, 0, 0)),
        compiler_params=_params("parallel"),
        name="moba_block_means",
    )(proj)


def _moba_select(gates, own):
    lane = lax.broadcasted_iota(jnp.int32, gates.shape, 1)
    g = jnp.where(lane < own, gates, NEG_INF)
    sel = jnp.where(lane == own, 1.0, 0.0)
    for _ in range(MOBA_TOPK):
        mx = jnp.max(g, axis=1, keepdims=True)
        pick = jnp.min(jnp.where(g == mx, lane, gates.shape[1]), axis=1, keepdims=True)
        hit = jnp.logical_and(lane == pick, mx > NEG_INF)
        sel = jnp.where(hit, 1.0, sel)
        g = jnp.where(lane == pick, NEG_INF, g)
    return sel


def _attn_b_kernel(q_ref, kmean_ref, k_ref, v_ref, o_ref, *scratch, t):
    qi = pl.program_id(1)
    per_tile = t // MOBA_BLOCK
    q_raw = q_ref[...]
    gates = lax.dot_general(q_raw, kmean_ref[...], _NT, precision=lax.Precision.HIGHEST,
                            preferred_element_type=F32)
    row1 = lax.broadcasted_iota(jnp.int32, (t, 1), 0)
    own = qi * per_tile + _shift_div(row1, MOBA_BLOCK)
    sel = _moba_select(gates, own)
    q = jnp.concatenate([(q_raw * (DH ** -0.5 * LOG2E)).astype(BF16),
                         jnp.where(sel > 0.5, 0.0, MASK_BIAS).astype(BF16)], axis=1)

    def block(ref, j):
        return ref[pl.ds(pl.multiple_of(j * t, t), t), :]

    def scores(j):
        key = lax.broadcasted_iota(jnp.int32, (t, sel.shape[1]), 0)
        blk = lax.broadcasted_iota(jnp.int32, (t, sel.shape[1]), 1)
        onehot = (blk == j * per_tile + _shift_div(key, MOBA_BLOCK)).astype(BF16)
        return _dot_nt(q, jnp.concatenate([block(k_ref, j), onehot], axis=1))

    def causal(j):
        row = lax.broadcasted_iota(jnp.int32, (t, t), 0)
        col = lax.broadcasted_iota(jnp.int32, (t, t), 1)
        return col <= row

    _flash_loop([(0, qi, None), (qi, qi + 1, causal)], qi + 1,
                scores, lambda j: block(v_ref, j), *scratch)
    m_ref, l_ref, acc_ref = scratch[:3]
    o_ref[...] = (acc_ref[...] / l_ref[...]).astype(o_ref.dtype)


def _attn_b(proj, proj_b, s):
    t = _pick(s, (512, 256))
    n_pad = -(-(s // MOBA_BLOCK) // V7X_LANES) * V7X_LANES
    kmean = _block_means(proj, s, n_pad)
    return pl.pallas_call(
        functools.partial(_attn_b_kernel, t=t),
        out_shape=jax.ShapeDtypeStruct((s, GROUP_W), BF16),
        grid=(N_HEADS, s // t),
        in_specs=[pl.BlockSpec((t, DH), lambda h, i: (i, COL_QB * N_HEADS + h)),
                  pl.BlockSpec((None, n_pad, DH), lambda h, i: (h, 0, 0)),
                  pl.BlockSpec((s, DH), lambda h, i: (0, COL_KB * N_HEADS + h)),
                  pl.BlockSpec((s, DH), lambda h, i: (0, COL_VB * N_HEADS + h))],
        out_specs=pl.BlockSpec((t, DH), lambda h, i: (i, h)),
        scratch_shapes=_flash_scratch(t, t),
        compiler_params=_params("parallel", "arbitrary"),
        name="prompt_moba",
    )(proj, kmean, proj_b, proj_b)


def _attn_d_kernel(q_ref, k_ref, v_ref, o_ref, r_ref, acc_ref, *, tq, tk):
    qi = pl.program_id(1)
    q = (q_ref[...] * (DH ** -0.5)).astype(BF16)
    r_ref[...] = jnp.zeros_like(r_ref)
    acc_ref[...] = jnp.zeros_like(acc_ref)

    def cond(carry):
        kb, r_max = carry
        return jnp.logical_and(kb >= 0, r_max > SB_DEAD)

    def body(carry):
        kb, _ = carry
        start = pl.multiple_of(kb * tk, tk)
        z = _dot_nt(q, k_ref[pl.ds(start, tk), :])
        row = lax.broadcasted_iota(jnp.int32, (tq, tk), 0)
        col = lax.broadcasted_iota(jnp.int32, (tq, tk), 1)
        strict = start + col < qi * tq + row
        sp = _softplus(z)
        lk = jnp.where(strict, -sp, 0.0)
        later = _later_sum(lk, _suffix_matrix(tk))
        r = r_ref[...]
        w = jnp.where(strict, jnp.exp(z - sp + later + r), 0.0)
        acc_ref[...] += _dot(w.astype(BF16), v_ref[pl.ds(start, tk), :])
        r_new = r + jnp.sum(lk, axis=1, keepdims=True)
        r_ref[...] = r_new
        return kb - 1, jnp.max(r_new)

    lax.while_loop(cond, body, ((qi * tq + tq - 1) // tk, jnp.float32(0.0)))
    o_ref[...] = acc_ref[...].astype(o_ref.dtype)


def _attn_d(proj, proj_b, s):
    tq = _pick(s, (256, 128))
    tk = _pick(s, (256, 128))
    return pl.pallas_call(
        functools.partial(_attn_d_kernel, tq=tq, tk=tk),
        out_shape=jax.ShapeDtypeStruct((s, GROUP_W), BF16),
        grid=(N_HEADS, s // tq),
        in_specs=[pl.BlockSpec((tq, DH), lambda h, i: (i, COL_QD * N_HEADS + h)),
                  pl.BlockSpec((s, DH), lambda h, i: (0, COL_KD * N_HEADS + h)),
                  pl.BlockSpec((s, DH), lambda h, i: (0, COL_VD * N_HEADS + h))],
        out_specs=pl.BlockSpec((tq, DH), lambda h, i: (i, h)),
        scratch_shapes=[pltpu.VMEM((tq, 1), F32), pltpu.VMEM((tq, DH), F32)],
        compiler_params=_params("parallel", "arbitrary"),
        name="prompt_stickbreak",
    )(proj, proj_b, proj_b)


def _conv_kernel(*refs, tt, has_prev):
    if has_prev:
        cur_ref, prev_ref, cbuf_ref, bglu_ref, wdw_ref, bdw_ref, gn_ref, bn_ref, o_ref, nbuf_ref, ubuf_ref = refs
    else:
        cur_ref, cbuf_ref, bglu_ref, wdw_ref, bdw_ref, gn_ref, bn_ref, o_ref, nbuf_ref, ubuf_ref = refs
    i = pl.program_id(1)
    c = GROUP_W

    def glu(x):
        x = x + bglu_ref[...]
        return x[:, :c] * jax.nn.sigmoid(x[:, c:])

    ubuf_ref[CONV_HALO:CONV_HALO + tt, :] = glu(cur_ref[...])

    @pl.when(i == 0)
    def _():
        ubuf_ref[CONV_HALO - (CONV_W - 1):CONV_HALO, :] = cbuf_ref[...]

    if has_prev:
        @pl.when(i > 0)
        def _():
            ubuf_ref[0:CONV_HALO, :] = glu(prev_ref[...])

    off = CONV_HALO - (CONV_W - 1)
    y = jnp.zeros((tt, c), F32) + bdw_ref[...]
    for j in range(CONV_W):
        y = y + ubuf_ref[off + j:off + j + tt, :] * wdw_ref[j:j + 1, :]

    gw = c // N_NORM_GROUPS
    for gi in range(N_NORM_GROUPS):
        seg = y[:, gi * gw:(gi + 1) * gw]
        mu = jnp.mean(seg, axis=-1, keepdims=True)
        var = jnp.mean(jnp.square(seg - mu), axis=-1, keepdims=True)
        yn = (seg - mu) * lax.rsqrt(var + EPS)
        yn = yn * gn_ref[:, gi * gw:(gi + 1) * gw] + bn_ref[:, gi * gw:(gi + 1) * gw]
        o_ref[:, gi * gw:(gi + 1) * gw] = (yn * jax.nn.sigmoid(yn)).astype(o_ref.dtype)

    @pl.when(i == pl.num_programs(1) - 1)
    def _():
        nbuf_ref[...] = ubuf_ref[CONV_HALO + tt - (CONV_W - 1):CONV_HALO + tt, :]


def _conv_group(proj, batch, s, conv_buf, b_glu, w_dw, b_dw, g_norm, b_norm, out_dtype):
    tt = _pick(s, _SMALL_ROW_TILES)
    nt = s // tt
    has_prev = nt > 1
    assert tt + CONV_HALO >= CONV_W - 1 and (not has_prev or tt % CONV_HALO == 0)
    c = GROUP_W
    glu_col = COL_GLU_A // 2
    in_specs = [pl.BlockSpec((tt, 2 * c), lambda b, i: (b * nt + i, glu_col))]
    args = [proj]
    if has_prev:
        per = tt // CONV_HALO
        in_specs.append(pl.BlockSpec((CONV_HALO, 2 * c),
                                     lambda b, i: (jnp.maximum((b * nt + i) * per - 1, 0), glu_col)))
        args.append(proj)
    vec = lambda n: pl.BlockSpec((1, n), lambda b, i: (0, 0))
    in_specs += [pl.BlockSpec((None, CONV_W - 1, c), lambda b, i: (b, 0, 0)),
                 vec(2 * c), pl.BlockSpec((CONV_W, c), lambda b, i: (0, 0)), vec(c), vec(c), vec(c)]
    args += [conv_buf, b_glu, w_dw, b_dw, g_norm, b_norm]
    return pl.pallas_call(
        functools.partial(_conv_kernel, tt=tt, has_prev=has_prev),
        out_shape=(jax.ShapeDtypeStruct((batch * s, c), out_dtype),
                   jax.ShapeDtypeStruct((batch, CONV_W - 1, c), F32)),
        grid=(batch, nt),
        in_specs=in_specs,
        out_specs=(pl.BlockSpec((tt, c), lambda b, i: (b * nt + i, 0)),
                   pl.BlockSpec((None, CONV_W - 1, c), lambda b, i: (b, 0, 0))),
        scratch_shapes=[pltpu.VMEM((CONV_HALO + tt, c), F32)],
        compiler_params=_params("parallel", "arbitrary"),
        name="conv_module",
    )(*args)


def _page_heads(ref, page, dtype):
    return [ref[pl.ds(h, page, stride=N_HEADS), :].astype(dtype) for h in range(N_HEADS)]


def _pad_heads(pad_ref, new_ref, dtype):
    pad_ref[...] = jnp.zeros_like(pad_ref)
    pad_ref[0:new_ref.shape[0], :] = new_ref[...]
    return [pad_ref[:, h * DH:(h + 1) * DH].astype(dtype) for h in range(N_HEADS)]


def _head_rows(fn):
    return jnp.concatenate([fn(h) for h in range(N_HEADS)], axis=0)


def _new_key_index(shape, nq):
    row = lax.broadcasted_iota(jnp.int32, shape, 0)
    col = lax.broadcasted_iota(jnp.int32, shape, 1)
    return col, row & (nq - 1)


def _dec_a_kernel(pt_ref, lam_ref, gsub_ref, q_ref, kn_ref, vn_ref, *rest, n_pg, nq, page, lam_init):
    del pt_ref
    k_pages, v_pages = rest[:n_pg], rest[n_pg:2 * n_pg]
    o_ref, qh_ref, kpad_ref, vpad_ref, m_ref, l_ref, acc_ref = rest[2 * n_pg:]
    g = pl.program_id(1)
    rh = 2 * nq

    def reduce(s, pv_fn):
        p, alpha = _softmax_update(s, m_ref, l_ref)
        pv = _head_rows(lambda h: pv_fn(h, p[h * rh:(h + 1) * rh].astype(BF16)))
        acc_ref[...] = alpha * acc_ref[...] + pv

    @pl.when(g == 0)
    def _():
        for h in range(N_HEADS):
            qh = q_ref[:, h * DH:(h + 1) * DH] * (DH_A ** -0.5 * LOG2E)
            qh_ref[h] = _split_maps(qh).astype(BF16)
        m_ref[...] = jnp.full_like(m_ref, M_INIT)
        l_ref[...] = jnp.zeros_like(l_ref)
        acc_ref[...] = jnp.zeros_like(acc_ref)
        kh = _pad_heads(kpad_ref, kn_ref, BF16)
        vh = _pad_heads(vpad_ref, vn_ref, BF16)
        s = _head_rows(lambda h: _dot_nt(qh_ref[h], kh[h]))
        key, qry = _new_key_index(s.shape, nq)
        reduce(jnp.where(key <= qry, s, NEG_INF), lambda h, ph: _dot(ph, vh[h]))

    kh = [_page_heads(r, page, BF16) for r in k_pages]
    vh = [_page_heads(r, page, BF16) for r in v_pages]
    s = jnp.concatenate([_head_rows(lambda h: _dot_nt(qh_ref[h], kh[t][h])) for t in range(n_pg)],
                        axis=1)
    reduce(s, lambda h, ph: sum(_dot(ph[:, t * page:(t + 1) * page], vh[t][h]) for t in range(n_pg)))

    @pl.when(g == pl.num_programs(1) - 1)
    def _():
        acc = acc_ref[...]
        l = l_ref[...]
        lam = _lambda(lam_ref, lam_init)
        for h in range(N_HEADS):
            r1, r2 = h * rh, h * rh + nq
            o = acc[r1:r1 + nq] / l[r1:r1 + nq] - lam * (acc[r2:r2 + nq] / l[r2:r2 + nq])
            o_ref[:, h * DH:(h + 1) * DH] = _rms(o, gsub_ref[...]) * (1.0 - lam_init)


def _dec_d_kernel(pt_ref, q_ref, kn_ref, vn_ref, *rest, n_pg, nq, page):
    del pt_ref
    k_pages, v_pages = rest[:n_pg], rest[n_pg:2 * n_pg]
    o_ref, qh_ref, kpad_ref, vpad_ref, r_ref, acc_ref, rmax_ref = rest[2 * n_pg:]
    g = pl.program_id(1)

    def step(kh, vh, strict, r):
        z = _head_rows(lambda h: _dot_nt(qh_ref[h], kh[h]))
        sp = _softplus(z)
        lk = -sp if strict is None else jnp.where(strict(z.shape), -sp, 0.0)
        later = _later_sum(lk, _suffix_matrix(page))
        w = jnp.exp(z - sp + later + r)
        if strict is not None:
            w = jnp.where(strict(z.shape), w, 0.0)
        acc_ref[...] += _head_rows(lambda h: _dot(w[h * nq:(h + 1) * nq].astype(BF16), vh[h]))
        r_new = r + jnp.sum(lk, axis=1, keepdims=True)
        r_ref[...] = r_new
        rmax_ref[0] = jnp.max(r_new)

    @pl.when(g == 0)
    def _():
        for h in range(N_HEADS):
            qh_ref[h] = (q_ref[:, h * DH:(h + 1) * DH] * (DH ** -0.5)).astype(BF16)
        acc_ref[...] = jnp.zeros_like(acc_ref)

        def strict(shape):
            key, qry = _new_key_index(shape, nq)
            return key < qry

        step(_pad_heads(kpad_ref, kn_ref, BF16), _pad_heads(vpad_ref, vn_ref, BF16), strict,
             jnp.zeros(r_ref.shape, F32))

    for t in range(n_pg):
        @pl.when(rmax_ref[0] > SB_DEAD)
        def _():
            step(_page_heads(k_pages[t], page, BF16), _page_heads(v_pages[t], page, BF16), None,
                 r_ref[...])

    @pl.when(g == pl.num_programs(1) - 1)
    def _():
        acc = acc_ref[...]
        for h in range(N_HEADS):
            o_ref[:, h * DH:(h + 1) * DH] = acc[h * nq:(h + 1) * nq]


def _dec_b_kernel(pt_ref, q_ref, kn_ref, vn_ref, *rest, n_pg, nq, page):
    del pt_ref
    k_pages, v_pages = rest[:n_pg], rest[n_pg:2 * n_pg]
    (o_ref, qf_ref, qh_ref, kpad_ref, vpad_ref,
     sg_ref, si_ref, sm_ref, sl_ref, sacc_ref) = rest[2 * n_pg:]
    g = pl.program_id(1)
    per_block = MOBA_BLOCK // page

    @pl.when(g == 0)
    def _():
        for h in range(N_HEADS):
            qh = q_ref[:, h * DH:(h + 1) * DH]
            qf_ref[h] = qh
            qh_ref[h] = (qh * (DH ** -0.5 * LOG2E)).astype(BF16)
        sg_ref[...] = jnp.full_like(sg_ref, NEG_INF)
        for slot in range(MOBA_TOPK):
            si_ref[slot] = jnp.full(si_ref.shape[1:], -1.0 - slot, F32)
        sm_ref[...] = jnp.full_like(sm_ref, NEG_INF)
        sl_ref[...] = jnp.zeros_like(sl_ref)
        sacc_ref[...] = jnp.zeros_like(sacc_ref)

    for blk in range(n_pg // per_block):
        kf = [_page_heads(k_pages[blk * per_block + t], page, F32) for t in range(per_block)]
        vh = [_page_heads(v_pages[blk * per_block + t], page, BF16) for t in range(per_block)]

        def gate_of(h):
            k_mean = sum(jnp.sum(kf[t][h], axis=0, keepdims=True) for t in range(per_block))
            return jnp.sum(qf_ref[h] * (k_mean * (1.0 / MOBA_BLOCK)), axis=1, keepdims=True)

        gate = _head_rows(gate_of)
        s = jnp.concatenate(
            [_head_rows(lambda h: _dot_nt(qh_ref[h], kf[t][h].astype(BF16))) for t in range(per_block)],
            axis=1)
        m_b = jnp.max(s, axis=1, keepdims=True)
        p = jnp.exp2(s - m_b)
        l_b = jnp.sum(p, axis=1, keepdims=True)
        acc_b = _head_rows(lambda h: sum(
            _dot(p[h * nq:(h + 1) * nq, t * page:(t + 1) * page].astype(BF16), vh[t][h])
            for t in range(per_block)))
        b_idx = (g * (n_pg // per_block) + blk).astype(F32)

        gs = [sg_ref[t] for t in range(MOBA_TOPK)]
        ids = [si_ref[t] for t in range(MOBA_TOPK)]

        def worse(a, b):
            return jnp.logical_or(gs[a] < gs[b], jnp.logical_and(gs[a] == gs[b], ids[a] > ids[b]))

        w0 = jnp.logical_and(worse(0, 1), worse(0, 2))
        w1 = jnp.logical_and(worse(1, 0), worse(1, 2))
        w2 = jnp.logical_not(jnp.logical_or(w0, w1))
        g_worst = jnp.where(w0, gs[0], jnp.where(w1, gs[1], gs[2]))
        replace = gate > g_worst
        for slot, is_worst in enumerate((w0, w1, w2)):
            take = jnp.logical_and(replace, is_worst)
            sg_ref[slot] = jnp.where(take, gate, gs[slot])
            si_ref[slot] = jnp.where(take, b_idx, ids[slot])
            sm_ref[slot] = jnp.where(take, m_b, sm_ref[slot])
            sl_ref[slot] = jnp.where(take, l_b, sl_ref[slot])
            sacc_ref[slot] = jnp.where(take, acc_b, sacc_ref[slot])

    @pl.when(g == pl.num_programs(1) - 1)
    def _():
        kh = _pad_heads(kpad_ref, kn_ref, BF16)
        vh = _pad_heads(vpad_ref, vn_ref, BF16)
        s = _head_rows(lambda h: _dot_nt(qh_ref[h], kh[h]))
        key, qry = _new_key_index(s.shape, nq)
        s = jnp.where(key <= qry, s, NEG_INF)
        m_o = jnp.max(s, axis=1, keepdims=True)
        p = jnp.exp2(s - m_o)
        m_all = m_o
        for slot in range(MOBA_TOPK):
            m_all = jnp.maximum(m_all, sm_ref[slot])
        w_o = jnp.exp2(m_o - m_all)
        num = w_o * _head_rows(lambda h: _dot(p[h * nq:(h + 1) * nq].astype(BF16), vh[h]))
        den = w_o * jnp.sum(p, axis=1, keepdims=True)
        for slot in range(MOBA_TOPK):
            w_s = jnp.exp2(sm_ref[slot] - m_all)
            num = num + w_s * sacc_ref[slot]
            den = den + w_s * sl_ref[slot]
        out = num / den
        for h in range(N_HEADS):
            o_ref[:, h * DH:(h + 1) * DH] = out[h * nq:(h + 1) * nq]


def _dec_attention(kind, proj, k_cache, v_cache, page_table, li, nq, cols, extra=(), **kw):
    n_seq, n_pages = page_table.shape
    page = k_cache.shape[2] // N_HEADS
    n_pg = _pick(n_pages, (8, 4, 2))
    steps = n_pages // n_pg
    col_q, col_k, col_v = cols
    w = GROUP_W
    assert nq % V7X_SUBLANES == 0 and nq & (nq - 1) == 0 and nq <= page
    assert (n_pages * page) % MOBA_BLOCK == 0 and MOBA_BLOCK % page == 0

    descending = kind == "d"

    def page_spec(t):
        def imap(b, g, pt):
            idx = g * n_pg + t
            if descending:
                idx = n_pages - 1 - idx
            return (li, pt[b, idx], 0, 0)
        return pl.BlockSpec((None, None, page * N_HEADS, DH), imap)

    small = lambda a: pl.BlockSpec(a.shape, lambda b, g, pt: (0,) * a.ndim)
    row_spec = lambda col: pl.BlockSpec((nq, w), lambda b, g, pt: (b, col))
    in_specs = ([small(a) for a in extra] + [row_spec(col_q), row_spec(col_k), row_spec(col_v)]
                + [page_spec(t) for t in range(n_pg)] * 2)
    args = list(extra) + [proj, proj, proj] + [k_cache] * n_pg + [v_cache] * n_pg

    pad = [pltpu.VMEM((page, w), F32), pltpu.VMEM((page, w), F32)]
    stat = lambda rows: pltpu.VMEM((rows, 1), F32)
    if kind == "a":
        rows = 2 * N_HEADS * nq
        body = functools.partial(_dec_a_kernel, n_pg=n_pg, nq=nq, page=page, **kw)
        scratch = [pltpu.VMEM((N_HEADS, 2 * nq, DH), BF16)] + pad + [
            stat(rows), stat(rows), pltpu.VMEM((rows, DH), F32)]
    elif kind == "d":
        rows = N_HEADS * nq
        body = functools.partial(_dec_d_kernel, n_pg=n_pg, nq=nq, page=page)
        scratch = [pltpu.VMEM((N_HEADS, nq, DH), BF16)] + pad + [
            stat(rows), pltpu.VMEM((rows, DH), F32), pltpu.SMEM((1,), F32)]
    else:
        rows = N_HEADS * nq
        body = functools.partial(_dec_b_kernel, n_pg=n_pg, nq=nq, page=page)
        slot = lambda width: pltpu.VMEM((MOBA_TOPK, rows, width), F32)
        scratch = [pltpu.VMEM((N_HEADS, nq, DH), F32), pltpu.VMEM((N_HEADS, nq, DH), BF16)] + pad + [
            slot(1), slot(1), slot(1), slot(1), slot(DH)]

    return pl.pallas_call(
        body,
        out_shape=jax.ShapeDtypeStruct((n_seq * nq, w), F32),
        grid_spec=pltpu.PrefetchScalarGridSpec(
            num_scalar_prefetch=1,
            grid=(n_seq, steps),
            in_specs=in_specs,
            out_specs=pl.BlockSpec((nq, w), lambda b, g, pt: (b, 0)),
            scratch_shapes=scratch),
        compiler_params=_params("parallel", "arbitrary"),
        name="sample_attn_" + kind,
    )(page_table, *args)


def _split_heads(proj, col, lead):
    return proj[:, col * GROUP_W:(col + 1) * GROUP_W].reshape(lead + (N_HEADS, DH))


def kernel(x_prompt, x_sample, cache_a_k, cache_a_v, cache_b_k, cache_b_v, cache_d_k, cache_d_v, state_c_conv, page_table, g_mix, w_in, lam_a, g_a_sub, b_c_glu, w_c_dw, b_c_dw, g_c_norm, b_c_norm, w_out, g_ffn, w_gate, w_up, w_down, g_final):
    n_b, s, d = x_prompt.shape
    n_seq, nq, _ = x_sample.shape
    depth = w_in.shape[0]
    assert n_b == 1, "prompt kernels keep one sequence's keys resident"

    def paged(c):
        assert c.shape[3:] == (N_HEADS, DH)
        return c.reshape(c.shape[0], c.shape[1], c.shape[2] * N_HEADS, DH)

    caches = [paged(c) for c in (cache_a_k, cache_a_v, cache_b_k, cache_b_v, cache_d_k, cache_d_v)]
    row = lambda v: v.reshape(1, -1)
    zero_buf = jnp.zeros((n_b, CONV_W - 1, GROUP_W), F32)

    xp = x_prompt.reshape(n_b * s, d)
    xs = x_sample.reshape(n_seq * nq, d)
    new_p = [[] for _ in range(7)]
    new_s = [[] for _ in range(7)]
    kv_cols = (COL_KA, COL_VA, COL_KB, COL_VB, COL_KD, COL_VD)

    for li in range(depth):
        w_in_l, w_out_l = w_in[li].astype(BF16), w_out[li].astype(BF16)
        wg_l, wu_l, wd_l = w_gate[li].astype(BF16), w_up[li].astype(BF16), w_down[li].astype(BF16)
        lam_init = 0.8 - 0.6 * math.exp(-0.3 * li)
        conv_w = (row(b_c_glu[li]), w_c_dw[li], row(b_c_dw[li]), row(g_c_norm[li]), row(b_c_norm[li]))
        last = li == depth - 1

        proj, proj_b = _norm_matmul(xp, row(g_mix[li]), w_in_l, with_bf16=True)
        o_a = _attn_a(proj, proj_b, s, lam_a[li], row(g_a_sub[li]), lam_init)
        o_b = _attn_b(proj, proj_b, s)
        o_c, buf_p = _conv_group(proj, n_b, s, zero_buf, *conv_w, out_dtype=BF16)
        o_d = _attn_d(proj, proj_b, s)
        xp = _out_proj(xp, (o_a, o_b, o_c, o_d), w_out_l)
        xp = _ffn(xp, row(g_ffn[li]), wg_l, wu_l, wd_l, row(g_final), final_norm=last)
        for acc, col in zip(new_p, kv_cols):
            acc.append(_split_heads(proj, col, (n_b, s)))
        new_p[6].append(buf_p)

        (proj,) = _norm_matmul(xs, row(g_mix[li]), w_in_l, with_bf16=False)
        o_a = _dec_attention("a", proj, caches[0], caches[1], page_table, li, nq,
                             (COL_QA, COL_KA, COL_VA), extra=(lam_a[li], row(g_a_sub[li])),
                             lam_init=lam_init)
        o_b = _dec_attention("b", proj, caches[2], caches[3], page_table, li, nq,
                             (COL_QB, COL_KB, COL_VB))
        o_c, buf_s = _conv_group(proj, n_seq, nq, state_c_conv[li], *conv_w, out_dtype=F32)
        o_d = _dec_attention("d", proj, caches[4], caches[5], page_table, li, nq,
                             (COL_QD, COL_KD, COL_VD))
        xs = _out_proj(xs, (o_a, o_b, o_c, o_d), w_out_l)
        xs = _ffn(xs, row(g_ffn[li]), wg_l, wu_l, wd_l, row(g_final), final_norm=last)
        for acc, col in zip(new_s, kv_cols):
            acc.append(_split_heads(proj, col, (n_seq, nq)))
        new_s[6].append(buf_s)

    outs_p = [jnp.stack(v, axis=0) for v in new_p]
    outs_s = [jnp.stack(v, axis=0) for v in new_s]
    return (xp.reshape(n_b, s, d), xs.reshape(n_seq, nq, d), *outs_p, *outs_s)
```

```python
import functools
import math

import jax
import jax.numpy as jnp
from jax import lax
from jax.experimental import pallas as pl
from jax.experimental.pallas import tpu as pltpu

F32 = jnp.float32
BF16 = jnp.bfloat16

EPS = 1e-6
N_HEADS = 4
DH = 128
GROUP_W = N_HEADS * DH
DH_A = DH // 2
MOBA_BLOCK = 256
MOBA_TOPK = 3
CONV_W = 31
CONV_HALO = 32
N_NORM_GROUPS = 4
NEG_INF = float("-inf")
M_INIT = -1e30
MASK_BIAS = -2.0 ** 100
LOG2E = math.log2(math.e)
SB_DEAD = -150.0
V7X_LANES = 128
V7X_SUBLANES = 8
VMEM_LIMIT_BYTES = 56 * 1024 * 1024

COL_QA, COL_KA, COL_VA, COL_QB, COL_KB, COL_VB, COL_GLU_A, COL_GLU_G, COL_QD, COL_KD, COL_VD = range(11)

_NT = (((1,), (1,)), ((), ()))
_SMALL_ROW_TILES = (512, 256, 128, 64, 32, 16, 8)


def _dot(a, b):
    return jnp.dot(a, b, preferred_element_type=F32)


def _dot_nt(a, b):
    return lax.dot_general(a, b, _NT, preferred_element_type=F32)


def _rms(x, g):
    return x * lax.rsqrt(jnp.mean(x * x, axis=-1, keepdims=True) + EPS) * g


def _shift_div(x, n):
    assert n & (n - 1) == 0
    return lax.shift_right_logical(x, n.bit_length() - 1)


def _pick(n, candidates):
    for c in candidates:
        if n % c == 0:
            return c
    raise ValueError(f"no tile for {n} in {candidates}")


def _params(*sem):
    return pltpu.CompilerParams(dimension_semantics=sem, vmem_limit_bytes=VMEM_LIMIT_BYTES)


def _norm_matmul_kernel(x_ref, g_ref, w_ref, o_ref, *rest):
    h_ref = rest[-1]

    @pl.when(pl.program_id(1) == 0)
    def _():
        h_ref[...] = _rms(x_ref[...], g_ref[...]).astype(BF16)

    y = _dot(h_ref[...], w_ref[...])
    o_ref[...] = y
    if len(rest) == 2:
        rest[0][...] = y.astype(BF16)


def _norm_matmul(x, g, w, with_bf16):
    t, d = x.shape
    n = w.shape[1]
    tm = _pick(t, (1024,) + _SMALL_ROW_TILES)
    tn = _pick(n, (1408, 512, 256, 128))
    out_spec = pl.BlockSpec((tm, tn), lambda i, j: (i, j))
    out_shape = [jax.ShapeDtypeStruct((t, n), F32)]
    if with_bf16:
        out_shape.append(jax.ShapeDtypeStruct((t, n), BF16))
    return pl.pallas_call(
        _norm_matmul_kernel,
        out_shape=out_shape,
        grid=(t // tm, n // tn),
        in_specs=[pl.BlockSpec((tm, d), lambda i, j: (i, 0)),
                  pl.BlockSpec((1, d), lambda i, j: (0, 0)),
                  pl.BlockSpec((d, tn), lambda i, j: (0, j))],
        out_specs=[out_spec] * len(out_shape),
        scratch_shapes=[pltpu.VMEM((tm, d), BF16)],
        compiler_params=_params("parallel", "arbitrary"),
        name="norm_in_proj",
    )(x, g, w)


def _out_proj_kernel(x_ref, a_ref, b_ref, c_ref, d_ref, w_ref, o_ref):
    acc = x_ref[...]
    for gi, m_ref in enumerate((a_ref, b_ref, c_ref, d_ref)):
        acc = acc + _dot(m_ref[...].astype(BF16), w_ref[gi * GROUP_W:(gi + 1) * GROUP_W, :])
    o_ref[...] = acc


def _out_proj(x, mixes, w):
    t, d = x.shape
    tm = _pick(t, (1024,) + _SMALL_ROW_TILES)
    mix_spec = pl.BlockSpec((tm, GROUP_W), lambda i: (i, 0))
    return pl.pallas_call(
        _out_proj_kernel,
        out_shape=jax.ShapeDtypeStruct((t, d), F32),
        grid=(t // tm,),
        in_specs=[pl.BlockSpec((tm, d), lambda i: (i, 0)), mix_spec, mix_spec, mix_spec, mix_spec,
                  pl.BlockSpec(w.shape, lambda i: (0, 0))],
        out_specs=pl.BlockSpec((tm, d), lambda i: (i, 0)),
        compiler_params=_params("parallel"),
        name="out_proj",
    )(x, *mixes, w)


def _ffn_kernel(x_ref, g_ref, wg_ref, wu_ref, wd_ref, gf_ref, o_ref, h_ref, acc_ref, *, final_norm):
    j = pl.program_id(1)

    @pl.when(j == 0)
    def _():
        h_ref[...] = _rms(x_ref[...], g_ref[...]).astype(BF16)
        acc_ref[...] = jnp.zeros_like(acc_ref)

    h = h_ref[...]
    gate = _dot(h, wg_ref[...])
    up = _dot(h, wu_ref[...])
    act = (gate * jax.nn.sigmoid(gate) * up).astype(BF16)
    acc_ref[...] += _dot(act, wd_ref[...])

    @pl.when(j == pl.num_programs(1) - 1)
    def _():
        y = x_ref[...] + acc_ref[...]
        if final_norm:
            y = _rms(y, gf_ref[...])
        o_ref[...] = y


def _ffn(x, g, wg, wu, wd, g_final, final_norm):
    t, d = x.shape
    f = wg.shape[1]
    tm = _pick(t, _SMALL_ROW_TILES)
    tf = _pick(f, (512, 256, 128))
    return pl.pallas_call(
        functools.partial(_ffn_kernel, final_norm=final_norm),
        out_shape=jax.ShapeDtypeStruct((t, d), F32),
        grid=(t // tm, f // tf),
        in_specs=[pl.BlockSpec((tm, d), lambda i, j: (i, 0)),
                  pl.BlockSpec((1, d), lambda i, j: (0, 0)),
                  pl.BlockSpec((d, tf), lambda i, j: (0, j)),
                  pl.BlockSpec((d, tf), lambda i, j: (0, j)),
                  pl.BlockSpec((tf, d), lambda i, j: (j, 0)),
                  pl.BlockSpec((1, d), lambda i, j: (0, 0))],
        out_specs=pl.BlockSpec((tm, d), lambda i, j: (i, 0)),
        scratch_shapes=[pltpu.VMEM((tm, d), BF16), pltpu.VMEM((tm, d), F32)],
        compiler_params=_params("parallel", "arbitrary"),
        name="ffn",
    )(x, g, wg, wu, wd, g_final)


def _lambda(lam_ref, lam_init):
    lp = lam_ref[...]
    s01 = jnp.sum(lp[0:1, :] * lp[1:2, :], axis=1, keepdims=True)
    s23 = jnp.sum(lp[2:3, :] * lp[3:4, :], axis=1, keepdims=True)
    return jnp.exp(s01) - jnp.exp(s23) + lam_init


def _split_maps(q):
    lane = lax.broadcasted_iota(jnp.int32, q.shape, 1)
    return jnp.concatenate([jnp.where(lane < DH_A, q, 0.0), jnp.where(lane >= DH_A, q, 0.0)], axis=0)


def _softmax_update(s, m_ref, l_ref):
    m_prev = m_ref[...]
    m_new = jnp.maximum(m_prev, jnp.max(s, axis=1, keepdims=True))
    alpha = jnp.exp2(m_prev - m_new)
    p = jnp.exp2(s - m_new)
    l_ref[...] = alpha * l_ref[...] + jnp.sum(p, axis=1, keepdims=True)
    m_ref[...] = m_new
    return p, alpha


def _flash_loop(segments, n_total, scores_fn, values_fn, m_ref, acc_ref, s_ref, p_ref, a_ref):
    reps = s_ref.shape[2] // V7X_LANES
    m_ref[...] = jnp.full_like(m_ref, M_INIT)
    acc_ref[...] = jnp.zeros_like(acc_ref)
    p_ref[...] = jnp.zeros_like(p_ref)
    a_ref[...] = jnp.ones_like(a_ref)
    s_ref[0] = scores_fn(0)

    def finish(j):
        acc_ref[:, :DH] = a_ref[...] * acc_ref[:, :DH] + _dot(p_ref[...], values_fn(j))

    def make_body(keep_fn):
        def body(j, carry):
            finish(jnp.maximum(j - 1, 0))
            slot = j & 1
            s = s_ref[slot]
            s_ref[1 - slot] = scores_fn(jnp.minimum(j + 1, n_total - 1))
            if keep_fn is not None:
                s = jnp.where(keep_fn(j), s, NEG_INF)
            m_prev = m_ref[...]
            m_new = jnp.maximum(m_prev, jnp.max(s, axis=1, keepdims=True))
            alpha = jnp.exp2(m_prev - m_new)
            p = jnp.exp2(s - jnp.concatenate([m_new] * reps, axis=1))
            acc_ref[:, DH:] = alpha * acc_ref[:, DH:] + jnp.sum(p, axis=1, keepdims=True)
            a_ref[...] = alpha
            m_ref[...] = m_new
            p_ref[...] = p.astype(BF16)
            return carry
        return body

    for lo, hi, keep_fn in segments:
        lax.fori_loop(lo, hi, make_body(keep_fn), 0)
    finish(n_total - 1)
    acc = acc_ref[...]
    return acc[:, :DH], acc[:, DH:]


def _flash_scratch(rows, tk):
    return [pltpu.VMEM((rows, V7X_LANES), F32), pltpu.VMEM((rows, 2 * DH), F32),
            pltpu.VMEM((2, rows, tk), F32), pltpu.VMEM((rows, tk), BF16),
            pltpu.VMEM((rows, V7X_LANES), F32)]


def _softplus(z):
    return jnp.maximum(z, 0.0) + jnp.log(1.0 + jnp.exp(-jnp.abs(z)))


def _later_sum(lk, upper):
    hi = lk.astype(BF16)
    lo = (lk - hi.astype(F32)).astype(BF16)
    return _dot(hi, upper) + _dot(lo, upper)


def _suffix_matrix(n):
    j = lax.broadcasted_iota(jnp.int32, (n, n), 0)
    s = lax.broadcasted_iota(jnp.int32, (n, n), 1)
    return (j > s).astype(BF16)


def _attn_a_kernel(lam_ref, gsub_ref, q_ref, k_ref, v_ref, o_ref, *scratch, tq, tk, lam_init):
    qi = pl.program_id(1)
    qq = _split_maps(q_ref[...] * (DH_A ** -0.5 * LOG2E)).astype(BF16)

    def block(ref, j):
        return ref[pl.ds(pl.multiple_of(j * tk, tk), tk), :]

    def causal(j):
        row = lax.broadcasted_iota(jnp.int32, (2 * tq, tk), 0)
        col = lax.broadcasted_iota(jnp.int32, (2 * tq, tk), 1)
        q_pos = qi * tq + jnp.where(row >= tq, row - tq, row)
        return j * tk + col <= q_pos

    n_full = (qi * tq) // tk
    n_total = (qi * tq + tq - 1) // tk + 1
    acc, l = _flash_loop([(0, n_full, None), (n_full, n_total, causal)], n_total,
                         lambda j: _dot_nt(qq, block(k_ref, j)), lambda j: block(v_ref, j), *scratch)
    lam = _lambda(lam_ref, lam_init)
    o = acc[:tq] / l[:tq] - lam * (acc[tq:] / l[tq:])
    o_ref[...] = (_rms(o, gsub_ref[...]) * (1.0 - lam_init)).astype(o_ref.dtype)


def _attn_a(proj, proj_b, s, lam_a, g_sub, lam_init):
    tq = _pick(s, (256, 128))
    tk = _pick(s, (512, 256, 128))
    return pl.pallas_call(
        functools.partial(_attn_a_kernel, tq=tq, tk=tk, lam_init=lam_init),
        out_shape=jax.ShapeDtypeStruct((s, GROUP_W), BF16),
        grid=(N_HEADS, s // tq),
        in_specs=[pl.BlockSpec(lam_a.shape, lambda h, i: (0, 0)),
                  pl.BlockSpec((1, DH), lambda h, i: (0, 0)),
                  pl.BlockSpec((tq, DH), lambda h, i: (i, COL_QA * N_HEADS + h)),
                  pl.BlockSpec((s, DH), lambda h, i: (0, COL_KA * N_HEADS + h)),
                  pl.BlockSpec((s, DH), lambda h, i: (0, COL_VA * N_HEADS + h))],
        out_specs=pl.BlockSpec((tq, DH), lambda h, i: (i, h)),
        scratch_shapes=_flash_scratch(2 * tq, tk),
        compiler_params=_params("parallel", "arbitrary"),
        name="prompt_diff_attn",
    )(lam_a, g_sub, proj, proj_b, proj_b)


def _block_mean_kernel(k_ref, o_ref, *, n_blocks):
    o_ref[...] = jnp.zeros_like(o_ref)

    def body(b, carry):
        rows = k_ref[pl.ds(pl.multiple_of(b * MOBA_BLOCK, MOBA_BLOCK), MOBA_BLOCK), :]
        o_ref[pl.ds(b, 1), :] = jnp.mean(rows, axis=0, keepdims=True)
        return carry

    lax.fori_loop(0, n_blocks, body, 0)


def _block_means(proj, s, n_pad):
    return pl.pallas_call(
        functools.partial(_block_mean_kernel, n_blocks=s // MOBA_BLOCK),
        out_shape=jax.ShapeDtypeStruct((N_HEADS, n_pad, DH), F32),
        grid=(N_HEADS,),
        in_specs=[pl.BlockSpec((s, DH), lambda h: (0, COL_KB * N_HEADS + h))],
        out_specs=pl.BlockSpec((None, n_pad, DH), lambda h: (h, 0, 0)),
        compiler_params=_params("parallel"),
        name="moba_block_means",
    )(proj)


def _moba_select(gates, own):
    lane = lax.broadcasted_iota(jnp.int32, gates.shape, 1)
    g = jnp.where(lane < own, gates, NEG_INF)
    sel = jnp.where(lane == own, 1.0, 0.0)
    for _ in range(MOBA_TOPK):
        mx = jnp.max(g, axis=1, keepdims=True)
        pick = jnp.min(jnp.where(g == mx, lane, gates.shape[1]), axis=1, keepdims=True)
        hit = jnp.logical_and(lane == pick, mx > NEG_INF)
        sel = jnp.where(hit, 1.0, sel)
        g = jnp.where(lane == pick, NEG_INF, g)
    return sel


def _attn_b_kernel(q_ref, kmean_ref, k_ref, v_ref, o_ref, *scratch, t):
    qi = pl.program_id(1)
    per_tile = t // MOBA_BLOCK
    q_raw = q_ref[...]
    gates = lax.dot_general(q_raw, kmean_ref[...], _NT, precision=lax.Precision.HIGHEST,
                            preferred_element_type=F32)
    row1 = lax.broadcasted_iota(jnp.int32, (t, 1), 0)
    own = qi * per_tile + _shift_div(row1, MOBA_BLOCK)
    sel = _moba_select(gates, own)
    q = jnp.concatenate([(q_raw * (DH ** -0.5 * LOG2E)).astype(BF16),
                         jnp.where(sel > 0.5, 0.0, MASK_BIAS).astype(BF16)], axis=1)

    def block(ref, j):
        return ref[pl.ds(pl.multiple_of(j * t, t), t), :]

    def scores(j):
        key = lax.broadcasted_iota(jnp.int32, (t, sel.shape[1]), 0)
        blk = lax.broadcasted_iota(jnp.int32, (t, sel.shape[1]), 1)
        onehot = (blk == j * per_tile + _shift_div(key, MOBA_BLOCK)).astype(BF16)
        return _dot_nt(q, jnp.concatenate([block(k_ref, j), onehot], axis=1))

    def causal(j):
        row = lax.broadcasted_iota(jnp.int32, (t, t), 0)
        col = lax.broadcasted_iota(jnp.int32, (t, t), 1)
        return col <= row

    acc, l = _flash_loop([(0, qi, None), (qi, qi + 1, causal)], qi + 1,
                         scores, lambda j: block(v_ref, j), *scratch)
    o_ref[...] = (acc / l).astype(o_ref.dtype)


def _attn_b(proj, proj_b, s):
    t = _pick(s, (512, 256))
    n_pad = -(-(s // MOBA_BLOCK) // V7X_LANES) * V7X_LANES
    kmean = _block_means(proj, s, n_pad)
    return pl.pallas_call(
        functools.partial(_attn_b_kernel, t=t),
        out_shape=jax.ShapeDtypeStruct((s, GROUP_W), BF16),
        grid=(N_HEADS, s // t),
        in_specs=[pl.BlockSpec((t, DH), lambda h, i: (i, COL_QB * N_HEADS + h)),
                  pl.BlockSpec((None, n_pad, DH), lambda h, i: (h, 0, 0)),
                  pl.BlockSpec((s, DH), lambda h, i: (0, COL_KB * N_HEADS + h)),
                  pl.BlockSpec((s, DH), lambda h, i: (0, COL_VB * N_HEADS + h))],
        out_specs=pl.BlockSpec((t, DH), lambda h, i: (i, h)),
        scratch_shapes=_flash_scratch(t, t),
        compiler_params=_params("parallel", "arbitrary"),
        name="prompt_moba",
    )(proj, kmean, proj_b, proj_b)


def _attn_d_kernel(q_ref, k_ref, v_ref, o_ref, r_ref, acc_ref, *, tq, tk):
    qi = pl.program_id(1)
    q = (q_ref[...] * (DH ** -0.5)).astype(BF16)
    r_ref[...] = jnp.zeros_like(r_ref)
    acc_ref[...] = jnp.zeros_like(acc_ref)

    def cond(carry):
        kb, r_max = carry
        return jnp.logical_and(kb >= 0, r_max > SB_DEAD)

    def body(carry):
        kb, _ = carry
        start = pl.multiple_of(kb * tk, tk)
        z = _dot_nt(q, k_ref[pl.ds(start, tk), :])
        row = lax.broadcasted_iota(jnp.int32, (tq, tk), 0)
        col = lax.broadcasted_iota(jnp.int32, (tq, tk), 1)
        strict = start + col < qi * tq + row
        sp = _softplus(z)
        lk = jnp.where(strict, -sp, 0.0)
        later = _later_sum(lk, _suffix_matrix(tk))
        r = r_ref[...]
        w = jnp.where(strict, jnp.exp(z - sp + later + r), 0.0)
        acc_ref[...] += _dot(w.astype(BF16), v_ref[pl.ds(start, tk), :])
        r_new = r + jnp.sum(lk, axis=1, keepdims=True)
        r_ref[...] = r_new
        return kb - 1, jnp.max(r_new)

    lax.while_loop(cond, body, ((qi * tq + tq - 1) // tk, jnp.float32(0.0)))
    o_ref[...] = acc_ref[...].astype(o_ref.dtype)


def _attn_d(proj, proj_b, s):
    tq = _pick(s, (256, 128))
    tk = _pick(s, (256, 128))
    return pl.pallas_call(
        functools.partial(_attn_d_kernel, tq=tq, tk=tk),
        out_shape=jax.ShapeDtypeStruct((s, GROUP_W), BF16),
        grid=(N_HEADS, s // tq),
        in_specs=[pl.BlockSpec((tq, DH), lambda h, i: (i, COL_QD * N_HEADS + h)),
                  pl.BlockSpec((s, DH), lambda h, i: (0, COL_KD * N_HEADS + h)),
                  pl.BlockSpec((s, DH), lambda h, i: (0, COL_VD * N_HEADS + h))],
        out_specs=pl.BlockSpec((tq, DH), lambda h, i: (i, h)),
        scratch_shapes=[pltpu.VMEM((tq, 1), F32), pltpu.VMEM((tq, DH), F32)],
        compiler_params=_params("parallel", "arbitrary"),
        name="prompt_stickbreak",
    )(proj, proj_b, proj_b)


def _conv_kernel(*refs, tt, has_prev):
    if has_prev:
        cur_ref, prev_ref, cbuf_ref, bglu_ref, wdw_ref, bdw_ref, gn_ref, bn_ref, o_ref, nbuf_ref, ubuf_ref = refs
    else:
        cur_ref, cbuf_ref, bglu_ref, wdw_ref, bdw_ref, gn_ref, bn_ref, o_ref, nbuf_ref, ubuf_ref = refs
    i = pl.program_id(1)
    c = GROUP_W

    def glu(x):
        x = x + bglu_ref[...]
        return x[:, :c] * jax.nn.sigmoid(x[:, c:])

    ubuf_ref[CONV_HALO:CONV_HALO + tt, :] = glu(cur_ref[...])

    @pl.when(i == 0)
    def _():
        ubuf_ref[CONV_HALO - (CONV_W - 1):CONV_HALO, :] = cbuf_ref[...]

    if has_prev:
        @pl.when(i > 0)
        def _():
            ubuf_ref[0:CONV_HALO, :] = glu(prev_ref[...])

    off = CONV_HALO - (CONV_W - 1)
    y = jnp.zeros((tt, c), F32) + bdw_ref[...]
    for j in range(CONV_W):
        y = y + ubuf_ref[off + j:off + j + tt, :] * wdw_ref[j:j + 1, :]

    gw = c // N_NORM_GROUPS
    for gi in range(N_NORM_GROUPS):
        seg = y[:, gi * gw:(gi + 1) * gw]
        mu = jnp.mean(seg, axis=-1, keepdims=True)
        var = jnp.mean(jnp.square(seg - mu), axis=-1, keepdims=True)
        yn = (seg - mu) * lax.rsqrt(var + EPS)
        yn = yn * gn_ref[:, gi * gw:(gi + 1) * gw] + bn_ref[:, gi * gw:(gi + 1) * gw]
        o_ref[:, gi * gw:(gi + 1) * gw] = (yn * jax.nn.sigmoid(yn)).astype(o_ref.dtype)

    @pl.when(i == pl.num_programs(1) - 1)
    def _():
        nbuf_ref[...] = ubuf_ref[CONV_HALO + tt - (CONV_W - 1):CONV_HALO + tt, :]


def _conv_group(proj, batch, s, conv_buf, b_glu, w_dw, b_dw, g_norm, b_norm, out_dtype):
    tt = _pick(s, _SMALL_ROW_TILES)
    nt = s // tt
    has_prev = nt > 1
    assert tt + CONV_HALO >= CONV_W - 1 and (not has_prev or tt % CONV_HALO == 0)
    c = GROUP_W
    glu_col = COL_GLU_A // 2
    in_specs = [pl.BlockSpec((tt, 2 * c), lambda b, i: (b * nt + i, glu_col))]
    args = [proj]
    if has_prev:
        per = tt // CONV_HALO
        in_specs.append(pl.BlockSpec((CONV_HALO, 2 * c),
                                     lambda b, i: (jnp.maximum((b * nt + i) * per - 1, 0), glu_col)))
        args.append(proj)
    vec = lambda n: pl.BlockSpec((1, n), lambda b, i: (0, 0))
    in_specs += [pl.BlockSpec((None, CONV_W - 1, c), lambda b, i: (b, 0, 0)),
                 vec(2 * c), pl.BlockSpec((CONV_W, c), lambda b, i: (0, 0)), vec(c), vec(c), vec(c)]
    args += [conv_buf, b_glu, w_dw, b_dw, g_norm, b_norm]
    return pl.pallas_call(
        functools.partial(_conv_kernel, tt=tt, has_prev=has_prev),
        out_shape=(jax.ShapeDtypeStruct((batch * s, c), out_dtype),
                   jax.ShapeDtypeStruct((batch, CONV_W - 1, c), F32)),
        grid=(batch, nt),
        in_specs=in_specs,
        out_specs=(pl.BlockSpec((tt, c), lambda b, i: (b * nt + i, 0)),
                   pl.BlockSpec((None, CONV_W - 1, c), lambda b, i: (b, 0, 0))),
        scratch_shapes=[pltpu.VMEM((CONV_HALO + tt, c), F32)],
        compiler_params=_params("parallel", "arbitrary"),
        name="conv_module",
    )(*args)


def _page_heads(ref, page, dtype):
    return [ref[pl.ds(h, page, stride=N_HEADS), :].astype(dtype) for h in range(N_HEADS)]


def _pad_heads(pad_ref, new_ref, dtype):
    pad_ref[...] = jnp.zeros_like(pad_ref)
    pad_ref[0:new_ref.shape[0], :] = new_ref[...]
    return [pad_ref[:, h * DH:(h + 1) * DH].astype(dtype) for h in range(N_HEADS)]


def _head_rows(fn):
    return jnp.concatenate([fn(h) for h in range(N_HEADS)], axis=0)


def _new_key_index(shape, nq):
    row = lax.broadcasted_iota(jnp.int32, shape, 0)
    col = lax.broadcasted_iota(jnp.int32, shape, 1)
    return col, row & (nq - 1)


def _dec_a_kernel(pt_ref, lam_ref, gsub_ref, q_ref, kn_ref, vn_ref, *rest, n_pg, nq, page, lam_init):
    del pt_ref
    k_pages, v_pages = rest[:n_pg], rest[n_pg:2 * n_pg]
    o_ref, qh_ref, kpad_ref, vpad_ref, m_ref, l_ref, acc_ref = rest[2 * n_pg:]
    g = pl.program_id(1)
    rh = 2 * nq

    def reduce(s, pv_fn):
        p, alpha = _softmax_update(s, m_ref, l_ref)
        pv = _head_rows(lambda h: pv_fn(h, p[h * rh:(h + 1) * rh].astype(BF16)))
        acc_ref[...] = alpha * acc_ref[...] + pv

    @pl.when(g == 0)
    def _():
        for h in range(N_HEADS):
            qh = q_ref[:, h * DH:(h + 1) * DH] * (DH_A ** -0.5 * LOG2E)
            qh_ref[h] = _split_maps(qh).astype(BF16)
        m_ref[...] = jnp.full_like(m_ref, M_INIT)
        l_ref[...] = jnp.zeros_like(l_ref)
        acc_ref[...] = jnp.zeros_like(acc_ref)
        kh = _pad_heads(kpad_ref, kn_ref, BF16)
        vh = _pad_heads(vpad_ref, vn_ref, BF16)
        s = _head_rows(lambda h: _dot_nt(qh_ref[h], kh[h]))
        key, qry = _new_key_index(s.shape, nq)
        reduce(jnp.where(key <= qry, s, NEG_INF), lambda h, ph: _dot(ph, vh[h]))

    kh = [_page_heads(r, page, BF16) for r in k_pages]
    vh = [_page_heads(r, page, BF16) for r in v_pages]
    s = jnp.concatenate([_head_rows(lambda h: _dot_nt(qh_ref[h], kh[t][h])) for t in range(n_pg)],
                        axis=1)
    reduce(s, lambda h, ph: sum(_dot(ph[:, t * page:(t + 1) * page], vh[t][h]) for t in range(n_pg)))

    @pl.when(g == pl.num_programs(1) - 1)
    def _():
        acc = acc_ref[...]
        l = l_ref[...]
        lam = _lambda(lam_ref, lam_init)
        for h in range(N_HEADS):
            r1, r2 = h * rh, h * rh + nq
            o = acc[r1:r1 + nq] / l[r1:r1 + nq] - lam * (acc[r2:r2 + nq] / l[r2:r2 + nq])
            o_ref[:, h * DH:(h + 1) * DH] = _rms(o, gsub_ref[...]) * (1.0 - lam_init)


def _dec_d_kernel(pt_ref, q_ref, kn_ref, vn_ref, *rest, n_pg, nq, page):
    del pt_ref
    k_pages, v_pages = rest[:n_pg], rest[n_pg:2 * n_pg]
    o_ref, qh_ref, kpad_ref, vpad_ref, r_ref, acc_ref, rmax_ref = rest[2 * n_pg:]
    g = pl.program_id(1)

    def step(kh, vh, strict, r):
        z = _head_rows(lambda h: _dot_nt(qh_ref[h], kh[h]))
        sp = _softplus(z)
        lk = -sp if strict is None else jnp.where(strict(z.shape), -sp, 0.0)
        later = _later_sum(lk, _suffix_matrix(page))
        w = jnp.exp(z - sp + later + r)
        if strict is not None:
            w = jnp.where(strict(z.shape), w, 0.0)
        acc_ref[...] += _head_rows(lambda h: _dot(w[h * nq:(h + 1) * nq].astype(BF16), vh[h]))
        r_new = r + jnp.sum(lk, axis=1, keepdims=True)
        r_ref[...] = r_new
        rmax_ref[0] = jnp.max(r_new)

    @pl.when(g == 0)
    def _():
        for h in range(N_HEADS):
            qh_ref[h] = (q_ref[:, h * DH:(h + 1) * DH] * (DH ** -0.5)).astype(BF16)
        acc_ref[...] = jnp.zeros_like(acc_ref)

        def strict(shape):
            key, qry = _new_key_index(shape, nq)
            return key < qry

        step(_pad_heads(kpad_ref, kn_ref, BF16), _pad_heads(vpad_ref, vn_ref, BF16), strict,
             jnp.zeros(r_ref.shape, F32))

    for t in range(n_pg):
        @pl.when(rmax_ref[0] > SB_DEAD)
        def _():
            step(_page_heads(k_pages[t], page, BF16), _page_heads(v_pages[t], page, BF16), None,
                 r_ref[...])

    @pl.when(g == pl.num_programs(1) - 1)
    def _():
        acc = acc_ref[...]
        for h in range(N_HEADS):
            o_ref[:, h * DH:(h + 1) * DH] = acc[h * nq:(h + 1) * nq]


def _dec_b_kernel(pt_ref, q_ref, kn_ref, vn_ref, *rest, n_pg, nq, page):
    del pt_ref
    k_pages, v_pages = rest[:n_pg], rest[n_pg:2 * n_pg]
    (o_ref, qf_ref, qh_ref, kpad_ref, vpad_ref,
     sg_ref, si_ref, sm_ref, sl_ref, sacc_ref) = rest[2 * n_pg:]
    g = pl.program_id(1)
    per_block = MOBA_BLOCK // page

    @pl.when(g == 0)
    def _():
        for h in range(N_HEADS):
            qh = q_ref[:, h * DH:(h + 1) * DH]
            qf_ref[h] = qh
            qh_ref[h] = (qh * (DH ** -0.5 * LOG2E)).astype(BF16)
        sg_ref[...] = jnp.full_like(sg_ref, NEG_INF)
        for slot in range(MOBA_TOPK):
            si_ref[slot] = jnp.full(si_ref.shape[1:], -1.0 - slot, F32)
        sm_ref[...] = jnp.full_like(sm_ref, NEG_INF)
        sl_ref[...] = jnp.zeros_like(sl_ref)
        sacc_ref[...] = jnp.zeros_like(sacc_ref)

    for blk in range(n_pg // per_block):
        kf = [_page_heads(k_pages[blk * per_block + t], page, F32) for t in range(per_block)]
        vh = [_page_heads(v_pages[blk * per_block + t], page, BF16) for t in range(per_block)]

        def gate_of(h):
            k_mean = sum(jnp.sum(kf[t][h], axis=0, keepdims=True) for t in range(per_block))
            return jnp.sum(qf_ref[h] * (k_mean * (1.0 / MOBA_BLOCK)), axis=1, keepdims=True)

        gate = _head_rows(gate_of)
        s = jnp.concatenate(
            [_head_rows(lambda h: _dot_nt(qh_ref[h], kf[t][h].astype(BF16))) for t in range(per_block)],
            axis=1)
        m_b = jnp.max(s, axis=1, keepdims=True)
        p = jnp.exp2(s - m_b)
        l_b = jnp.sum(p, axis=1, keepdims=True)
        acc_b = _head_rows(lambda h: sum(
            _dot(p[h * nq:(h + 1) * nq, t * page:(t + 1) * page].astype(BF16), vh[t][h])
            for t in range(per_block)))
        b_idx = (g * (n_pg // per_block) + blk).astype(F32)

        gs = [sg_ref[t] for t in range(MOBA_TOPK)]
        ids = [si_ref[t] for t in range(MOBA_TOPK)]

        def worse(a, b):
            return jnp.logical_or(gs[a] < gs[b], jnp.logical_and(gs[a] == gs[b], ids[a] > ids[b]))

        w0 = jnp.logical_and(worse(0, 1), worse(0, 2))
        w1 = jnp.logical_and(worse(1, 0), worse(1, 2))
        w2 = jnp.logical_not(jnp.logical_or(w0, w1))
        g_worst = jnp.where(w0, gs[0], jnp.where(w1, gs[1], gs[2]))
        replace = gate > g_worst
        for slot, is_worst in enumerate((w0, w1, w2)):
            take = jnp.logical_and(replace, is_worst)
            sg_ref[slot] = jnp.where(take, gate, gs[slot])
            si_ref[slot] = jnp.where(take, b_idx, ids[slot])
            sm_ref[slot] = jnp.where(take, m_b, sm_ref[slot])
            sl_ref[slot] = jnp.where(take, l_b, sl_ref[slot])
            sacc_ref[slot] = jnp.where(take, acc_b, sacc_ref[slot])

    @pl.when(g == pl.num_programs(1) - 1)
    def _():
        kh = _pad_heads(kpad_ref, kn_ref, BF16)
        vh = _pad_heads(vpad_ref, vn_ref, BF16)
        s = _head_rows(lambda h: _dot_nt(qh_ref[h], kh[h]))
        key, qry = _new_key_index(s.shape, nq)
        s = jnp.where(key <= qry, s, NEG_INF)
        m_o = jnp.max(s, axis=1, keepdims=True)
        p = jnp.exp2(s - m_o)
        m_all = m_o
        for slot in range(MOBA_TOPK):
            m_all = jnp.maximum(m_all, sm_ref[slot])
        w_o = jnp.exp2(m_o - m_all)
        num = w_o * _head_rows(lambda h: _dot(p[h * nq:(h + 1) * nq].astype(BF16), vh[h]))
        den = w_o * jnp.sum(p, axis=1, keepdims=True)
        for slot in range(MOBA_TOPK):
            w_s = jnp.exp2(sm_ref[slot] - m_all)
            num = num + w_s * sacc_ref[slot]
            den = den + w_s * sl_ref[slot]
        out = num / den
        for h in range(N_HEADS):
            o_ref[:, h * DH:(h + 1) * DH] = out[h * nq:(h + 1) * nq]


def _dec_attention(kind, proj, k_cache, v_cache, page_table, li, nq, cols, extra=(), **kw):
    n_seq, n_pages = page_table.shape
    page = k_cache.shape[2] // N_HEADS
    n_pg = _pick(n_pages, (16, 8, 4, 2))
    steps = n_pages // n_pg
    col_q, col_k, col_v = cols
    w = GROUP_W
    assert nq % V7X_SUBLANES == 0 and nq & (nq - 1) == 0 and nq <= page
    assert (n_pages * page) % MOBA_BLOCK == 0 and MOBA_BLOCK % page == 0

    descending = kind == "d"

    def page_spec(t):
        def imap(b, g, pt):
            idx = g * n_pg + t
            if descending:
                idx = n_pages - 1 - idx
            return (li, pt[b, idx], 0, 0)
        return pl.BlockSpec((None, None, page * N_HEADS, DH), imap)

    small = lambda a: pl.BlockSpec(a.shape, lambda b, g, pt: (0,) * a.ndim)
    row_spec = lambda col: pl.BlockSpec((nq, w), lambda b, g, pt: (b, col))
    in_specs = ([small(a) for a in extra] + [row_spec(col_q), row_spec(col_k), row_spec(col_v)]
                + [page_spec(t) for t in range(n_pg)] * 2)
    args = list(extra) + [proj, proj, proj] + [k_cache] * n_pg + [v_cache] * n_pg

    pad = [pltpu.VMEM((page, w), F32), pltpu.VMEM((page, w), F32)]
    stat = lambda rows: pltpu.VMEM((rows, 1), F32)
    if kind == "a":
        rows = 2 * N_HEADS * nq
        body = functools.partial(_dec_a_kernel, n_pg=n_pg, nq=nq, page=page, **kw)
        scratch = [pltpu.VMEM((N_HEADS, 2 * nq, DH), BF16)] + pad + [
            stat(rows), stat(rows), pltpu.VMEM((rows, DH), F32)]
    elif kind == "d":
        rows = N_HEADS * nq
        body = functools.partial(_dec_d_kernel, n_pg=n_pg, nq=nq, page=page)
        scratch = [pltpu.VMEM((N_HEADS, nq, DH), BF16)] + pad + [
            stat(rows), pltpu.VMEM((rows, DH), F32), pltpu.SMEM((1,), F32)]
    else:
        rows = N_HEADS * nq
        body = functools.partial(_dec_b_kernel, n_pg=n_pg, nq=nq, page=page)
        slot = lambda width: pltpu.VMEM((MOBA_TOPK, rows, width), F32)
        scratch = [pltpu.VMEM((N_HEADS, nq, DH), F32), pltpu.VMEM((N_HEADS, nq, DH), BF16)] + pad + [
            slot(1), slot(1), slot(1), slot(1), slot(DH)]

    return pl.pallas_call(
        body,
        out_shape=jax.ShapeDtypeStruct((n_seq * nq, w), F32),
        grid_spec=pltpu.PrefetchScalarGridSpec(
            num_scalar_prefetch=1,
            grid=(n_seq, steps),
            in_specs=in_specs,
            out_specs=pl.BlockSpec((nq, w), lambda b, g, pt: (b, 0)),
            scratch_shapes=scratch),
        compiler_params=_params("parallel", "arbitrary"),
        name="sample_attn_" + kind,
    )(page_table, *args)


KV_COLS = (COL_KA, COL_VA, COL_KB, COL_VB, COL_KD, COL_VD)


def _new_rows_kernel(*refs, n_layers, tm):
    n_in = n_layers * len(KV_COLS)
    ins, outs = refs[:n_in], refs[n_in:]
    for li in range(n_layers):
        @pl.when(pl.program_id(0) == li)
        def _():
            for gi, o_ref in enumerate(outs):
                src = ins[li * len(KV_COLS) + gi]
                for h in range(N_HEADS):
                    o_ref[pl.ds(h, tm, stride=N_HEADS), :] = src[:, h * DH:(h + 1) * DH]


def _new_rows(projs):
    n_layers = len(projs)
    t = projs[0].shape[0]
    tm = _pick(t, _SMALL_ROW_TILES)
    nt = t // tm

    def in_spec(li, col):
        def imap(d, i):
            return (jnp.where(d == li, i, jnp.where(d > li, nt - 1, 0)), col)
        return pl.BlockSpec((tm, GROUP_W), imap)

    in_specs = [in_spec(li, col) for li in range(n_layers) for col in KV_COLS]
    args = [projs[li] for li in range(n_layers) for _ in KV_COLS]
    out_shape = [jax.ShapeDtypeStruct((n_layers, t * N_HEADS, DH), F32)] * len(KV_COLS)
    out_specs = [pl.BlockSpec((None, tm * N_HEADS, DH), lambda d, i: (d, i, 0))] * len(KV_COLS)
    return pl.pallas_call(
        functools.partial(_new_rows_kernel, n_layers=n_layers, tm=tm),
        out_shape=out_shape,
        grid=(n_layers, nt),
        in_specs=in_specs,
        out_specs=out_specs,
        compiler_params=_params("arbitrary", "arbitrary"),
        name="new_kv_rows",
    )(*args)


def kernel(x_prompt, x_sample, cache_a_k, cache_a_v, cache_b_k, cache_b_v, cache_d_k, cache_d_v, state_c_conv, page_table, g_mix, w_in, lam_a, g_a_sub, b_c_glu, w_c_dw, b_c_dw, g_c_norm, b_c_norm, w_out, g_ffn, w_gate, w_up, w_down, g_final):
    n_b, s, d = x_prompt.shape
    n_seq, nq, _ = x_sample.shape
    depth = w_in.shape[0]
    assert n_b == 1, "prompt kernels keep one sequence's keys resident"

    def paged(c):
        assert c.shape[3:] == (N_HEADS, DH)
        return c.reshape(c.shape[0], c.shape[1], c.shape[2] * N_HEADS, DH)

    caches = [paged(c) for c in (cache_a_k, cache_a_v, cache_b_k, cache_b_v, cache_d_k, cache_d_v)]
    row = lambda v: v.reshape(1, -1)
    zero_buf = jnp.zeros((n_b, CONV_W - 1, GROUP_W), F32)

    xp = x_prompt.reshape(n_b * s, d)
    xs = x_sample.reshape(n_seq * nq, d)
    projs_p, projs_s, bufs_p, bufs_s = [], [], [], []

    for li in range(depth):
        w_in_l, w_out_l = w_in[li].astype(BF16), w_out[li].astype(BF16)
        wg_l, wu_l, wd_l = w_gate[li].astype(BF16), w_up[li].astype(BF16), w_down[li].astype(BF16)
        lam_init = 0.8 - 0.6 * math.exp(-0.3 * li)
        conv_w = (row(b_c_glu[li]), w_c_dw[li], row(b_c_dw[li]), row(g_c_norm[li]), row(b_c_norm[li]))
        last = li == depth - 1

        proj, proj_b = _norm_matmul(xp, row(g_mix[li]), w_in_l, with_bf16=True)
        o_a = _attn_a(proj, proj_b, s, lam_a[li], row(g_a_sub[li]), lam_init)
        o_b = _attn_b(proj, proj_b, s)
        o_c, buf_p = _conv_group(proj, n_b, s, zero_buf, *conv_w, out_dtype=BF16)
        o_d = _attn_d(proj, proj_b, s)
        xp = _out_proj(xp, (o_a, o_b, o_c, o_d), w_out_l)
        xp = _ffn(xp, row(g_ffn[li]), wg_l, wu_l, wd_l, row(g_final), final_norm=last)
        projs_p.append(proj)
        bufs_p.append(buf_p)

        (proj,) = _norm_matmul(xs, row(g_mix[li]), w_in_l, with_bf16=False)
        o_a = _dec_attention("a", proj, caches[0], caches[1], page_table, li, nq,
                             (COL_QA, COL_KA, COL_VA), extra=(lam_a[li], row(g_a_sub[li])),
                             lam_init=lam_init)
        o_b = _dec_attention("b", proj, caches[2], caches[3], page_table, li, nq,
                             (COL_QB, COL_KB, COL_VB))
        o_c, buf_s = _conv_group(proj, n_seq, nq, state_c_conv[li], *conv_w, out_dtype=F32)
        o_d = _dec_attention("d", proj, caches[4], caches[5], page_table, li, nq,
                             (COL_QD, COL_KD, COL_VD))
        xs = _out_proj(xs, (o_a, o_b, o_c, o_d), w_out_l)
        xs = _ffn(xs, row(g_ffn[li]), wg_l, wu_l, wd_l, row(g_final), final_norm=last)
        projs_s.append(proj)
        bufs_s.append(buf_s)

    outs_p = [o.reshape(depth, n_b, s, N_HEADS, DH) for o in _new_rows(projs_p)]
    outs_s = [o.reshape(depth, n_seq, nq, N_HEADS, DH) for o in _new_rows(projs_s)]
    return (xp.reshape(n_b, s, d), xs.reshape(n_seq, nq, d),
            *outs_p, jnp.stack(bufs_p, axis=0), *outs_s, jnp.stack(bufs_s, axis=0))
```

```python
import functools
import math

import jax
import jax.numpy as jnp
from jax import lax
from jax.experimental import pallas as pl
from jax.experimental.pallas import tpu as pltpu

F32 = jnp.float32
BF16 = jnp.bfloat16

EPS = 1e-6
N_HEADS = 4
DH = 128
GROUP_W = N_HEADS * DH
DH_A = DH // 2
MOBA_BLOCK = 256
MOBA_TOPK = 3
CONV_W = 31
CONV_HALO = 32
N_NORM_GROUPS = 4
NEG_INF = float("-inf")
M_INIT = -1e30
MASK_BIAS = -2.0 ** 100
LOG2E = math.log2(math.e)
SB_DEAD = -150.0
V7X_LANES = 128
V7X_SUBLANES = 8
VMEM_LIMIT_BYTES = 56 * 1024 * 1024

COL_QA, COL_KA, COL_VA, COL_QB, COL_KB, COL_VB, COL_GLU_A, COL_GLU_G, COL_QD, COL_KD, COL_VD = range(11)

_NT = (((1,), (1,)), ((), ()))
_SMALL_ROW_TILES = (512, 256, 128, 64, 32, 16, 8)


def _dot(a, b):
    return jnp.dot(a, b, preferred_element_type=F32)


def _dot_nt(a, b):
    return lax.dot_general(a, b, _NT, preferred_element_type=F32)


def _rms(x, g):
    return x * lax.rsqrt(jnp.mean(x * x, axis=-1, keepdims=True) + EPS) * g


def _shift_div(x, n):
    assert n & (n - 1) == 0
    return lax.shift_right_logical(x, n.bit_length() - 1)


def _pick(n, candidates):
    for c in candidates:
        if n % c == 0:
            return c
    raise ValueError(f"no tile for {n} in {candidates}")


def _params(*sem):
    return pltpu.CompilerParams(dimension_semantics=sem, vmem_limit_bytes=VMEM_LIMIT_BYTES)


def _norm_matmul_kernel(x_ref, g_ref, w_ref, o_ref, *rest):
    h_ref = rest[-1]

    @pl.when(pl.program_id(1) == 0)
    def _():
        h_ref[...] = _rms(x_ref[...], g_ref[...]).astype(BF16)

    y = _dot(h_ref[...], w_ref[...])
    o_ref[...] = y
    if len(rest) == 2:
        rest[0][...] = y.astype(BF16)


def _norm_matmul(x, g, w, with_bf16):
    t, d = x.shape
    n = w.shape[1]
    tm = _pick(t, (1024,) + _SMALL_ROW_TILES)
    tn = _pick(n, (1408, 512, 256, 128))
    out_spec = pl.BlockSpec((tm, tn), lambda i, j: (i, j))
    out_shape = [jax.ShapeDtypeStruct((t, n), F32)]
    if with_bf16:
        out_shape.append(jax.ShapeDtypeStruct((t, n), BF16))
    return pl.pallas_call(
        _norm_matmul_kernel,
        out_shape=out_shape,
        grid=(t // tm, n // tn),
        in_specs=[pl.BlockSpec((tm, d), lambda i, j: (i, 0)),
                  pl.BlockSpec((1, d), lambda i, j: (0, 0)),
                  pl.BlockSpec((d, tn), lambda i, j: (0, j))],
        out_specs=[out_spec] * len(out_shape),
        scratch_shapes=[pltpu.VMEM((tm, d), BF16)],
        compiler_params=_params("parallel", "arbitrary"),
        name="norm_in_proj",
    )(x, g, w)


def _out_proj_kernel(x_ref, a_ref, b_ref, c_ref, d_ref, w_ref, o_ref):
    acc = x_ref[...]
    for gi, m_ref in enumerate((a_ref, b_ref, c_ref, d_ref)):
        acc = acc + _dot(m_ref[...].astype(BF16), w_ref[gi * GROUP_W:(gi + 1) * GROUP_W, :])
    o_ref[...] = acc


def _out_proj(x, mixes, w):
    t, d = x.shape
    tm = _pick(t, (1024,) + _SMALL_ROW_TILES)
    mix_spec = pl.BlockSpec((tm, GROUP_W), lambda i: (i, 0))
    return pl.pallas_call(
        _out_proj_kernel,
        out_shape=jax.ShapeDtypeStruct((t, d), F32),
        grid=(t // tm,),
        in_specs=[pl.BlockSpec((tm, d), lambda i: (i, 0)), mix_spec, mix_spec, mix_spec, mix_spec,
                  pl.BlockSpec(w.shape, lambda i: (0, 0))],
        out_specs=pl.BlockSpec((tm, d), lambda i: (i, 0)),
        compiler_params=_params("parallel"),
        name="out_proj",
    )(x, *mixes, w)


def _ffn_kernel(x_ref, g_ref, wg_ref, wu_ref, wd_ref, gf_ref, o_ref, h_ref, acc_ref, *, final_norm):
    j = pl.program_id(1)

    @pl.when(j == 0)
    def _():
        h_ref[...] = _rms(x_ref[...], g_ref[...]).astype(BF16)
        acc_ref[...] = jnp.zeros_like(acc_ref)

    h = h_ref[...]
    gate = _dot(h, wg_ref[...])
    up = _dot(h, wu_ref[...])
    act = (gate * jax.nn.sigmoid(gate) * up).astype(BF16)
    acc_ref[...] += _dot(act, wd_ref[...])

    @pl.when(j == pl.num_programs(1) - 1)
    def _():
        y = x_ref[...] + acc_ref[...]
        if final_norm:
            y = _rms(y, gf_ref[...])
        o_ref[...] = y


def _ffn(x, g, wg, wu, wd, g_final, final_norm):
    t, d = x.shape
    f = wg.shape[1]
    tm = _pick(t, _SMALL_ROW_TILES)
    tf = _pick(f, (512, 256, 128))
    return pl.pallas_call(
        functools.partial(_ffn_kernel, final_norm=final_norm),
        out_shape=jax.ShapeDtypeStruct((t, d), F32),
        grid=(t // tm, f // tf),
        in_specs=[pl.BlockSpec((tm, d), lambda i, j: (i, 0)),
                  pl.BlockSpec((1, d), lambda i, j: (0, 0)),
                  pl.BlockSpec((d, tf), lambda i, j: (0, j)),
                  pl.BlockSpec((d, tf), lambda i, j: (0, j)),
                  pl.BlockSpec((tf, d), lambda i, j: (j, 0)),
                  pl.BlockSpec((1, d), lambda i, j: (0, 0))],
        out_specs=pl.BlockSpec((tm, d), lambda i, j: (i, 0)),
        scratch_shapes=[pltpu.VMEM((tm, d), BF16), pltpu.VMEM((tm, d), F32)],
        compiler_params=_params("parallel", "arbitrary"),
        name="ffn",
    )(x, g, wg, wu, wd, g_final)


def _lambda(lam_ref, lam_init):
    lp = lam_ref[...]
    s01 = jnp.sum(lp[0:1, :] * lp[1:2, :], axis=1, keepdims=True)
    s23 = jnp.sum(lp[2:3, :] * lp[3:4, :], axis=1, keepdims=True)
    return jnp.exp(s01) - jnp.exp(s23) + lam_init


def _split_maps(q):
    lane = lax.broadcasted_iota(jnp.int32, q.shape, 1)
    return jnp.concatenate([jnp.where(lane < DH_A, q, 0.0), jnp.where(lane >= DH_A, q, 0.0)], axis=0)


def _softmax_update(s, m_ref, l_ref):
    m_prev = m_ref[...]
    m_new = jnp.maximum(m_prev, jnp.max(s, axis=1, keepdims=True))
    alpha = jnp.exp2(m_prev - m_new)
    p = jnp.exp2(s - m_new)
    l_ref[...] = alpha * l_ref[...] + jnp.sum(p, axis=1, keepdims=True)
    m_ref[...] = m_new
    return p, alpha


def _flash_loop(segments, n_total, scores_fn, values_fn, m_ref, acc_ref, s_ref, p_ref, a_ref):
    reps = s_ref.shape[2] // V7X_LANES
    m_ref[...] = jnp.full_like(m_ref, M_INIT)
    acc_ref[...] = jnp.zeros_like(acc_ref)
    p_ref[...] = jnp.zeros_like(p_ref)
    a_ref[...] = jnp.ones_like(a_ref)
    s_ref[0] = scores_fn(0)

    def finish(j):
        acc_ref[:, :DH] = a_ref[...] * acc_ref[:, :DH] + _dot(p_ref[...], values_fn(j))

    def make_body(keep_fn):
        def body(j, carry):
            finish(jnp.maximum(j - 1, 0))
            slot = j & 1
            s = s_ref[slot]
            s_ref[1 - slot] = scores_fn(jnp.minimum(j + 1, n_total - 1))
            if keep_fn is not None:
                s = jnp.where(keep_fn(j), s, NEG_INF)
            m_prev = m_ref[...]
            m_new = jnp.maximum(m_prev, jnp.max(s, axis=1, keepdims=True))
            alpha = jnp.exp2(m_prev - m_new)
            p = jnp.exp2(s - jnp.concatenate([m_new] * reps, axis=1))
            acc_ref[:, DH:] = alpha * acc_ref[:, DH:] + jnp.sum(p, axis=1, keepdims=True)
            a_ref[...] = alpha
            m_ref[...] = m_new
            p_ref[...] = p.astype(BF16)
            return carry
        return body

    for lo, hi, keep_fn in segments:
        lax.fori_loop(lo, hi, make_body(keep_fn), 0)
    finish(n_total - 1)
    acc = acc_ref[...]
    return acc[:, :DH], acc[:, DH:]


def _flash_scratch(rows, tk):
    return [pltpu.VMEM((rows, V7X_LANES), F32), pltpu.VMEM((rows, 2 * DH), F32),
            pltpu.VMEM((2, rows, tk), F32), pltpu.VMEM((rows, tk), BF16),
            pltpu.VMEM((rows, V7X_LANES), F32)]


def _softplus(z):
    return jnp.maximum(z, 0.0) + jnp.log(1.0 + jnp.exp(-jnp.abs(z)))


def _later_sum(lk, upper):
    hi = lk.astype(BF16)
    lo = (lk - hi.astype(F32)).astype(BF16)
    return _dot(hi, upper) + _dot(lo, upper)


def _suffix_matrix(n):
    j = lax.broadcasted_iota(jnp.int32, (n, n), 0)
    s = lax.broadcasted_iota(jnp.int32, (n, n), 1)
    return (j > s).astype(BF16)


def _attn_a_kernel(lam_ref, gsub_ref, q_ref, k_ref, v_ref, o_ref, *scratch, tq, tk, lam_init):
    qi = pl.program_id(1)
    qq = _split_maps(q_ref[...] * (DH_A ** -0.5 * LOG2E)).astype(BF16)

    def block(ref, j):
        return ref[pl.ds(pl.multiple_of(j * tk, tk), tk), :]

    def causal(j):
        row = lax.broadcasted_iota(jnp.int32, (2 * tq, tk), 0)
        col = lax.broadcasted_iota(jnp.int32, (2 * tq, tk), 1)
        q_pos = qi * tq + jnp.where(row >= tq, row - tq, row)
        return j * tk + col <= q_pos

    n_full = (qi * tq) // tk
    n_total = (qi * tq + tq - 1) // tk + 1
    acc, l = _flash_loop([(0, n_full, None), (n_full, n_total, causal)], n_total,
                         lambda j: _dot_nt(qq, block(k_ref, j)), lambda j: block(v_ref, j), *scratch)
    lam = _lambda(lam_ref, lam_init)
    o = acc[:tq] / l[:tq] - lam * (acc[tq:] / l[tq:])
    o_ref[...] = (_rms(o, gsub_ref[...]) * (1.0 - lam_init)).astype(o_ref.dtype)


def _attn_a(proj, proj_b, s, lam_a, g_sub, lam_init):
    tq = _pick(s, (256, 128))
    tk = _pick(s, (512, 256, 128))
    return pl.pallas_call(
        functools.partial(_attn_a_kernel, tq=tq, tk=tk, lam_init=lam_init),
        out_shape=jax.ShapeDtypeStruct((s, GROUP_W), BF16),
        grid=(N_HEADS, s // tq),
        in_specs=[pl.BlockSpec(lam_a.shape, lambda h, i: (0, 0)),
                  pl.BlockSpec((1, DH), lambda h, i: (0, 0)),
                  pl.BlockSpec((tq, DH), lambda h, i: (i, COL_QA * N_HEADS + h)),
                  pl.BlockSpec((s, DH), lambda h, i: (0, COL_KA * N_HEADS + h)),
                  pl.BlockSpec((s, DH), lambda h, i: (0, COL_VA * N_HEADS + h))],
        out_specs=pl.BlockSpec((tq, DH), lambda h, i: (i, h)),
        scratch_shapes=_flash_scratch(2 * tq, tk),
        compiler_params=_params("parallel", "arbitrary"),
        name="prompt_diff_attn",
    )(lam_a, g_sub, proj, proj_b, proj_b)


def _block_mean_kernel(k_ref, o_ref, *, n_blocks):
    o_ref[...] = jnp.zeros_like(o_ref)

    def body(b, carry):
        rows = k_ref[pl.ds(pl.multiple_of(b * MOBA_BLOCK, MOBA_BLOCK), MOBA_BLOCK), :]
        o_ref[pl.ds(b, 1), :] = jnp.mean(rows, axis=0, keepdims=True)
        return carry

    lax.fori_loop(0, n_blocks, body, 0)


def _block_means(proj, s, n_pad):
    return pl.pallas_call(
        functools.partial(_block_mean_kernel, n_blocks=s // MOBA_BLOCK),
        out_shape=jax.ShapeDtypeStruct((N_HEADS, n_pad, DH), F32),
        grid=(N_HEADS,),
        in_specs=[pl.BlockSpec((s, DH), lambda h: (0, COL_KB * N_HEADS + h))],
        out_specs=pl.BlockSpec((None, n_pad, DH), lambda h: (h, 0, 0)),
        compiler_params=_params("parallel"),
        name="moba_block_means",
    )(proj)


def _moba_select(gates, own):
    lane = lax.broadcasted_iota(jnp.int32, gates.shape, 1)
    g = jnp.where(lane < own, gates, NEG_INF)
    sel = jnp.where(lane == own, 1.0, 0.0)
    for _ in range(MOBA_TOPK):
        mx = jnp.max(g, axis=1, keepdims=True)
        pick = jnp.min(jnp.where(g == mx, lane, gates.shape[1]), axis=1, keepdims=True)
        hit = jnp.logical_and(lane == pick, mx > NEG_INF)
        sel = jnp.where(hit, 1.0, sel)
        g = jnp.where(lane == pick, NEG_INF, g)
    return sel


def _attn_b_kernel(q_ref, kmean_ref, k_ref, v_ref, o_ref, *scratch, t):
    qi = pl.program_id(1)
    per_tile = t // MOBA_BLOCK
    q_raw = q_ref[...]
    gates = lax.dot_general(q_raw, kmean_ref[...], _NT, precision=lax.Precision.HIGHEST,
                            preferred_element_type=F32)
    row1 = lax.broadcasted_iota(jnp.int32, (t, 1), 0)
    own = qi * per_tile + _shift_div(row1, MOBA_BLOCK)
    sel = _moba_select(gates, own)
    q = jnp.concatenate([(q_raw * (DH ** -0.5 * LOG2E)).astype(BF16),
                         jnp.where(sel > 0.5, 0.0, MASK_BIAS).astype(BF16)], axis=1)

    def block(ref, j):
        return ref[pl.ds(pl.multiple_of(j * t, t), t), :]

    def scores(j):
        key = lax.broadcasted_iota(jnp.int32, (t, sel.shape[1]), 0)
        blk = lax.broadcasted_iota(jnp.int32, (t, sel.shape[1]), 1)
        onehot = (blk == j * per_tile + _shift_div(key, MOBA_BLOCK)).astype(BF16)
        return _dot_nt(q, jnp.concatenate([block(k_ref, j), onehot], axis=1))

    def causal(j):
        row = lax.broadcasted_iota(jnp.int32, (t, t), 0)
        col = lax.broadcasted_iota(jnp.int32, (t, t), 1)
        return col <= row

    acc, l = _flash_loop([(0, qi, None), (qi, qi + 1, causal)], qi + 1,
                         scores, lambda j: block(v_ref, j), *scratch)
    o_ref[...] = (acc / l).astype(o_ref.dtype)


def _attn_b(proj, proj_b, s):
    t = _pick(s, (512, 256))
    n_pad = -(-(s // MOBA_BLOCK) // V7X_LANES) * V7X_LANES
    kmean = _block_means(proj, s, n_pad)
    return pl.pallas_call(
        functools.partial(_attn_b_kernel, t=t),
        out_shape=jax.ShapeDtypeStruct((s, GROUP_W), BF16),
        grid=(N_HEADS, s // t),
        in_specs=[pl.BlockSpec((t, DH), lambda h, i: (i, COL_QB * N_HEADS + h)),
                  pl.BlockSpec((None, n_pad, DH), lambda h, i: (h, 0, 0)),
                  pl.BlockSpec((s, DH), lambda h, i: (0, COL_KB * N_HEADS + h)),
                  pl.BlockSpec((s, DH), lambda h, i: (0, COL_VB * N_HEADS + h))],
        out_specs=pl.BlockSpec((t, DH), lambda h, i: (i, h)),
        scratch_shapes=_flash_scratch(t, t),
        compiler_params=_params("parallel", "arbitrary"),
        name="prompt_moba",
    )(proj, kmean, proj_b, proj_b)


def _attn_d_kernel(q_ref, k_ref, v_ref, o_ref, r_ref, acc_ref, *, tq, tk):
    qi = pl.program_id(1)
    q = (q_ref[...] * (DH ** -0.5)).astype(BF16)
    r_ref[...] = jnp.zeros_like(r_ref)
    acc_ref[...] = jnp.zeros_like(acc_ref)

    def cond(carry):
        kb, r_max = carry
        return jnp.logical_and(kb >= 0, r_max > SB_DEAD)

    def body(carry):
        kb, _ = carry
        start = pl.multiple_of(kb * tk, tk)
        z = _dot_nt(q, k_ref[pl.ds(start, tk), :])
        row = lax.broadcasted_iota(jnp.int32, (tq, tk), 0)
        col = lax.broadcasted_iota(jnp.int32, (tq, tk), 1)
        strict = start + col < qi * tq + row
        sp = _softplus(z)
        lk = jnp.where(strict, -sp, 0.0)
        later = _later_sum(lk, _suffix_matrix(tk))
        r = r_ref[...]
        w = jnp.where(strict, jnp.exp(z - sp + later + r), 0.0)
        acc_ref[...] += _dot(w.astype(BF16), v_ref[pl.ds(start, tk), :])
        r_new = r + jnp.sum(lk, axis=1, keepdims=True)
        r_ref[...] = r_new
        return kb - 1, jnp.max(r_new)

    lax.while_loop(cond, body, ((qi * tq + tq - 1) // tk, jnp.float32(0.0)))
    o_ref[...] = acc_ref[...].astype(o_ref.dtype)


def _attn_d(proj, proj_b, s):
    tq = _pick(s, (256, 128))
    tk = _pick(s, (256, 128))
    return pl.pallas_call(
        functools.partial(_attn_d_kernel, tq=tq, tk=tk),
        out_shape=jax.ShapeDtypeStruct((s, GROUP_W), BF16),
        grid=(N_HEADS, s // tq),
        in_specs=[pl.BlockSpec((tq, DH), lambda h, i: (i, COL_QD * N_HEADS + h)),
                  pl.BlockSpec((s, DH), lambda h, i: (0, COL_KD * N_HEADS + h)),
                  pl.BlockSpec((s, DH), lambda h, i: (0, COL_VD * N_HEADS + h))],
        out_specs=pl.BlockSpec((tq, DH), lambda h, i: (i, h)),
        scratch_shapes=[pltpu.VMEM((tq, 1), F32), pltpu.VMEM((tq, DH), F32)],
        compiler_params=_params("parallel", "arbitrary"),
        name="prompt_stickbreak",
    )(proj, proj_b, proj_b)


def _conv_kernel(*refs, tt, has_prev):
    if has_prev:
        cur_ref, prev_ref, cbuf_ref, bglu_ref, wdw_ref, bdw_ref, gn_ref, bn_ref, o_ref, nbuf_ref, ubuf_ref = refs
    else:
        cur_ref, cbuf_ref, bglu_ref, wdw_ref, bdw_ref, gn_ref, bn_ref, o_ref, nbuf_ref, ubuf_ref = refs
    i = pl.program_id(1)
    c = GROUP_W

    def glu(x):
        x = x + bglu_ref[...]
        return x[:, :c] * jax.nn.sigmoid(x[:, c:])

    ubuf_ref[CONV_HALO:CONV_HALO + tt, :] = glu(cur_ref[...])

    @pl.when(i == 0)
    def _():
        ubuf_ref[CONV_HALO - (CONV_W - 1):CONV_HALO, :] = cbuf_ref[...]

    if has_prev:
        @pl.when(i > 0)
        def _():
            ubuf_ref[0:CONV_HALO, :] = glu(prev_ref[...])

    off = CONV_HALO - (CONV_W - 1)
    y = jnp.zeros((tt, c), F32) + bdw_ref[...]
    for j in range(CONV_W):
        y = y + ubuf_ref[off + j:off + j + tt, :] * wdw_ref[j:j + 1, :]

    gw = c // N_NORM_GROUPS
    for gi in range(N_NORM_GROUPS):
        seg = y[:, gi * gw:(gi + 1) * gw]
        mu = jnp.mean(seg, axis=-1, keepdims=True)
        var = jnp.mean(jnp.square(seg - mu), axis=-1, keepdims=True)
        yn = (seg - mu) * lax.rsqrt(var + EPS)
        yn = yn * gn_ref[:, gi * gw:(gi + 1) * gw] + bn_ref[:, gi * gw:(gi + 1) * gw]
        o_ref[:, gi * gw:(gi + 1) * gw] = (yn * jax.nn.sigmoid(yn)).astype(o_ref.dtype)

    @pl.when(i == pl.num_programs(1) - 1)
    def _():
        nbuf_ref[...] = ubuf_ref[CONV_HALO + tt - (CONV_W - 1):CONV_HALO + tt, :]


def _conv_group(proj, batch, s, conv_buf, b_glu, w_dw, b_dw, g_norm, b_norm, out_dtype):
    tt = _pick(s, _SMALL_ROW_TILES)
    nt = s // tt
    has_prev = nt > 1
    assert tt + CONV_HALO >= CONV_W - 1 and (not has_prev or tt % CONV_HALO == 0)
    c = GROUP_W
    glu_col = COL_GLU_A // 2
    in_specs = [pl.BlockSpec((tt, 2 * c), lambda b, i: (b * nt + i, glu_col))]
    args = [proj]
    if has_prev:
        per = tt // CONV_HALO
        in_specs.append(pl.BlockSpec((CONV_HALO, 2 * c),
                                     lambda b, i: (jnp.maximum((b * nt + i) * per - 1, 0), glu_col)))
        args.append(proj)
    vec = lambda n: pl.BlockSpec((1, n), lambda b, i: (0, 0))
    in_specs += [pl.BlockSpec((None, CONV_W - 1, c), lambda b, i: (b, 0, 0)),
                 vec(2 * c), pl.BlockSpec((CONV_W, c), lambda b, i: (0, 0)), vec(c), vec(c), vec(c)]
    args += [conv_buf, b_glu, w_dw, b_dw, g_norm, b_norm]
    return pl.pallas_call(
        functools.partial(_conv_kernel, tt=tt, has_prev=has_prev),
        out_shape=(jax.ShapeDtypeStruct((batch * s, c), out_dtype),
                   jax.ShapeDtypeStruct((batch, CONV_W - 1, c), F32)),
        grid=(batch, nt),
        in_specs=in_specs,
        out_specs=(pl.BlockSpec((tt, c), lambda b, i: (b * nt + i, 0)),
                   pl.BlockSpec((None, CONV_W - 1, c), lambda b, i: (b, 0, 0))),
        scratch_shapes=[pltpu.VMEM((CONV_HALO + tt, c), F32)],
        compiler_params=_params("parallel", "arbitrary"),
        name="conv_module",
    )(*args)


def _page_heads(ref, page, dtype):
    return [ref[pl.ds(h, page, stride=N_HEADS), :].astype(dtype) for h in range(N_HEADS)]


def _pad_heads(pad_ref, new_ref, dtype):
    pad_ref[...] = jnp.zeros_like(pad_ref)
    pad_ref[0:new_ref.shape[0], :] = new_ref[...]
    return [pad_ref[:, h * DH:(h + 1) * DH].astype(dtype) for h in range(N_HEADS)]


def _head_rows(fn):
    return jnp.concatenate([fn(h) for h in range(N_HEADS)], axis=0)


def _new_key_index(shape, nq):
    row = lax.broadcasted_iota(jnp.int32, shape, 0)
    col = lax.broadcasted_iota(jnp.int32, shape, 1)
    return col, row & (nq - 1)


def _dec_a_kernel(pt_ref, lam_ref, gsub_ref, q_ref, kn_ref, vn_ref, *rest, n_pg, nq, page, lam_init):
    del pt_ref
    k_pages, v_pages = rest[:n_pg], rest[n_pg:2 * n_pg]
    o_ref, qh_ref, kpad_ref, vpad_ref, m_ref, l_ref, acc_ref = rest[2 * n_pg:]
    g = pl.program_id(1)
    rh = 2 * nq

    def reduce(s, pv_fn):
        p, alpha = _softmax_update(s, m_ref, l_ref)
        pv = _head_rows(lambda h: pv_fn(h, p[h * rh:(h + 1) * rh].astype(BF16)))
        acc_ref[...] = alpha * acc_ref[...] + pv

    @pl.when(g == 0)
    def _():
        for h in range(N_HEADS):
            qh = q_ref[:, h * DH:(h + 1) * DH] * (DH_A ** -0.5 * LOG2E)
            qh_ref[h] = _split_maps(qh).astype(BF16)
        m_ref[...] = jnp.full_like(m_ref, M_INIT)
        l_ref[...] = jnp.zeros_like(l_ref)
        acc_ref[...] = jnp.zeros_like(acc_ref)
        kh = _pad_heads(kpad_ref, kn_ref, BF16)
        vh = _pad_heads(vpad_ref, vn_ref, BF16)
        s = _head_rows(lambda h: _dot_nt(qh_ref[h], kh[h]))
        key, qry = _new_key_index(s.shape, nq)
        reduce(jnp.where(key <= qry, s, NEG_INF), lambda h, ph: _dot(ph, vh[h]))

    kh = [_page_heads(r, page, BF16) for r in k_pages]
    vh = [_page_heads(r, page, BF16) for r in v_pages]
    s = jnp.concatenate([_head_rows(lambda h: _dot_nt(qh_ref[h], kh[t][h])) for t in range(n_pg)],
                        axis=1)
    reduce(s, lambda h, ph: sum(_dot(ph[:, t * page:(t + 1) * page], vh[t][h]) for t in range(n_pg)))

    @pl.when(g == pl.num_programs(1) - 1)
    def _():
        acc = acc_ref[...]
        l = l_ref[...]
        lam = _lambda(lam_ref, lam_init)
        for h in range(N_HEADS):
            r1, r2 = h * rh, h * rh + nq
            o = acc[r1:r1 + nq] / l[r1:r1 + nq] - lam * (acc[r2:r2 + nq] / l[r2:r2 + nq])
            o_ref[:, h * DH:(h + 1) * DH] = _rms(o, gsub_ref[...]) * (1.0 - lam_init)


def _stickbreak_pages(qh_ref, k_pages, v_pages, r_ref, acc_ref, rmax_ref, nq, page):
    def step(kh, vh, strict, r):
        z = _head_rows(lambda h: _dot_nt(qh_ref[h], kh[h]))
        sp = _softplus(z)
        lk = -sp if strict is None else jnp.where(strict(z.shape), -sp, 0.0)
        later = _later_sum(lk, _suffix_matrix(page))
        w = jnp.exp(z - sp + later + r)
        if strict is not None:
            w = jnp.where(strict(z.shape), w, 0.0)
        acc_ref[...] += _head_rows(lambda h: _dot(w[h * nq:(h + 1) * nq].astype(BF16), vh[h]))
        r_new = r + jnp.sum(lk, axis=1, keepdims=True)
        r_ref[...] = r_new
        rmax_ref[0] = jnp.max(r_new)

    def run_pages():
        for k_ref, v_ref in zip(k_pages, v_pages):
            @pl.when(rmax_ref[0] > SB_DEAD)
            def _():
                step(_page_heads(k_ref, page, BF16), _page_heads(v_ref, page, BF16), None, r_ref[...])

    return step, run_pages


def _dec_d_head_kernel(pt_ref, q_ref, kn_ref, vn_ref, *rest, n_pg, nq, page):
    del pt_ref
    k_pages, v_pages = rest[:n_pg], rest[n_pg:2 * n_pg]
    acc_out, r_out, qh_ref, kpad_ref, vpad_ref, r_ref, acc_ref, rmax_ref = rest[2 * n_pg:]
    step, run_pages = _stickbreak_pages(qh_ref, k_pages, v_pages, r_ref, acc_ref, rmax_ref, nq, page)
    for h in range(N_HEADS):
        qh_ref[h] = (q_ref[:, h * DH:(h + 1) * DH] * (DH ** -0.5)).astype(BF16)
    acc_ref[...] = jnp.zeros_like(acc_ref)

    def strict(shape):
        key, qry = _new_key_index(shape, nq)
        return key < qry

    step(_pad_heads(kpad_ref, kn_ref, BF16), _pad_heads(vpad_ref, vn_ref, BF16), strict,
         jnp.zeros(r_ref.shape, F32))
    run_pages()
    acc_out[...] = acc_ref[...]
    r_out[...] = jnp.broadcast_to(r_ref[...], r_out.shape)


def _dec_d_tail_kernel(pt_ref, dead_ref, q_ref, acc_in, r_in, *rest, n_pg, nq, page):
    del pt_ref
    k_pages, v_pages = rest[:n_pg], rest[n_pg:2 * n_pg]
    o_ref, qh_ref, r_ref, acc_ref, rmax_ref = rest[2 * n_pg:]
    g = pl.program_id(1)
    _, run_pages = _stickbreak_pages(qh_ref, k_pages, v_pages, r_ref, acc_ref, rmax_ref, nq, page)

    @pl.when(g == 0)
    def _():
        for h in range(N_HEADS):
            qh_ref[h] = (q_ref[:, h * DH:(h + 1) * DH] * (DH ** -0.5)).astype(BF16)
        acc_ref[...] = acc_in[...]
        r = r_in[:, 0:1]
        r_ref[...] = r
        rmax_ref[0] = jnp.where(dead_ref[pl.program_id(0)] == 1, NEG_INF, jnp.max(r))

    run_pages()

    @pl.when(g == pl.num_programs(1) - 1)
    def _():
        acc = acc_ref[...]
        for h in range(N_HEADS):
            o_ref[:, h * DH:(h + 1) * DH] = acc[h * nq:(h + 1) * nq]


def _dec_b_kernel(pt_ref, q_ref, kn_ref, vn_ref, *rest, n_pg, nq, page, n_blocks):
    del pt_ref
    k_pages, v_pages = rest[:n_pg], rest[n_pg:2 * n_pg]
    o_ref, qf_ref, qh_ref, kpad_ref, vpad_ref, g_ref, m_ref, l_ref, acc_ref = rest[2 * n_pg:]
    g = pl.program_id(1)
    per_block = MOBA_BLOCK // page
    lane = lax.broadcasted_iota(jnp.int32, g_ref.shape, 1)

    @pl.when(g == 0)
    def _():
        for h in range(N_HEADS):
            qh = q_ref[:, h * DH:(h + 1) * DH]
            qf_ref[h] = qh
            qh_ref[h] = (qh * (DH ** -0.5 * LOG2E)).astype(BF16)
        g_ref[...] = jnp.full_like(g_ref, NEG_INF)
        m_ref[...] = jnp.full_like(m_ref, M_INIT)
        l_ref[...] = jnp.zeros_like(l_ref)

    gates, maxes, sums = g_ref[...], m_ref[...], l_ref[...]
    for blk in range(n_pg // per_block):
        kf = [_page_heads(k_pages[blk * per_block + t], page, F32) for t in range(per_block)]
        vh = [_page_heads(v_pages[blk * per_block + t], page, BF16) for t in range(per_block)]

        def gate_of(h):
            k_mean = sum(jnp.sum(kf[t][h], axis=0, keepdims=True) for t in range(per_block))
            return jnp.sum(qf_ref[h] * (k_mean * (1.0 / MOBA_BLOCK)), axis=1, keepdims=True)

        gate = _head_rows(gate_of)
        s = jnp.concatenate(
            [_head_rows(lambda h: _dot_nt(qh_ref[h], kf[t][h].astype(BF16))) for t in range(per_block)],
            axis=1)
        m_b = jnp.max(s, axis=1, keepdims=True)
        p = jnp.exp2(s - m_b)
        l_b = jnp.sum(p, axis=1, keepdims=True)
        acc_b = _head_rows(lambda h: sum(
            _dot(p[h * nq:(h + 1) * nq, t * page:(t + 1) * page].astype(BF16), vh[t][h])
            for t in range(per_block)))
        b_idx = g * (n_pg // per_block) + blk
        acc_ref[b_idx] = acc_b
        here = lane == b_idx
        gates = jnp.where(here, gate, gates)
        maxes = jnp.where(here, m_b, maxes)
        sums = jnp.where(here, l_b, sums)
    g_ref[...], m_ref[...], l_ref[...] = gates, maxes, sums

    @pl.when(g == pl.num_programs(1) - 1)
    def _():
        kh = _pad_heads(kpad_ref, kn_ref, BF16)
        vh = _pad_heads(vpad_ref, vn_ref, BF16)
        s = _head_rows(lambda h: _dot_nt(qh_ref[h], kh[h]))
        key, qry = _new_key_index(s.shape, nq)
        s = jnp.where(key <= qry, s, NEG_INF)
        m_o = jnp.max(s, axis=1, keepdims=True)
        p = jnp.exp2(s - m_o)
        chosen = jnp.logical_and(_moba_select(gates, n_blocks) > 0.5, lane < n_blocks)
        m_all = jnp.maximum(m_o, jnp.max(jnp.where(chosen, maxes, NEG_INF), axis=1, keepdims=True))
        weight = jnp.where(chosen, jnp.exp2(maxes - m_all), 0.0)
        w_o = jnp.exp2(m_o - m_all)
        den = w_o * jnp.sum(p, axis=1, keepdims=True) + jnp.sum(weight * sums, axis=1, keepdims=True)

        def add_block(b, num):
            w_b = jnp.sum(jnp.where(lane == b, weight, 0.0), axis=1, keepdims=True)
            return num + w_b * acc_ref[b]

        num = lax.fori_loop(
            0, n_blocks, add_block,
            w_o * _head_rows(lambda h: _dot(p[h * nq:(h + 1) * nq].astype(BF16), vh[h])))
        out = num / den
        for h in range(N_HEADS):
            o_ref[:, h * DH:(h + 1) * DH] = out[h * nq:(h + 1) * nq]


def _dec_attention(kind, proj, k_cache, v_cache, page_table, li, nq, cols, extra=(), **kw):
    n_seq, n_pages = page_table.shape
    page = k_cache.shape[2] // N_HEADS
    n_pg = _pick(n_pages, (16, 8, 4, 2))
    steps = n_pages // n_pg
    col_q, col_k, col_v = cols
    w = GROUP_W
    assert nq % V7X_SUBLANES == 0 and nq & (nq - 1) == 0 and nq <= page
    assert (n_pages * page) % MOBA_BLOCK == 0 and MOBA_BLOCK % page == 0

    def page_spec(t):
        return pl.BlockSpec((None, None, page * N_HEADS, DH),
                            lambda b, g, pt: (li, pt[b, g * n_pg + t], 0, 0))

    small = lambda a: pl.BlockSpec(a.shape, lambda b, g, pt: (0,) * a.ndim)
    row_spec = lambda col: pl.BlockSpec((nq, w), lambda b, g, pt: (b, col))
    in_specs = ([small(a) for a in extra] + [row_spec(col_q), row_spec(col_k), row_spec(col_v)]
                + [page_spec(t) for t in range(n_pg)] * 2)
    args = list(extra) + [proj, proj, proj] + [k_cache] * n_pg + [v_cache] * n_pg

    pad = [pltpu.VMEM((page, w), F32), pltpu.VMEM((page, w), F32)]
    stat = lambda rows: pltpu.VMEM((rows, 1), F32)
    if kind == "a":
        rows = 2 * N_HEADS * nq
        body = functools.partial(_dec_a_kernel, n_pg=n_pg, nq=nq, page=page, **kw)
        scratch = [pltpu.VMEM((N_HEADS, 2 * nq, DH), BF16)] + pad + [
            stat(rows), stat(rows), pltpu.VMEM((rows, DH), F32)]
    else:
        rows = N_HEADS * nq
        n_blocks = n_pages * page // MOBA_BLOCK
        assert n_blocks <= V7X_LANES
        body = functools.partial(_dec_b_kernel, n_pg=n_pg, nq=nq, page=page, n_blocks=n_blocks)
        lanes = lambda: pltpu.VMEM((rows, V7X_LANES), F32)
        scratch = [pltpu.VMEM((N_HEADS, nq, DH), F32), pltpu.VMEM((N_HEADS, nq, DH), BF16)] + pad + [
            lanes(), lanes(), lanes(), pltpu.VMEM((n_blocks, rows, DH), F32)]

    return pl.pallas_call(
        body,
        out_shape=jax.ShapeDtypeStruct((n_seq * nq, w), F32),
        grid_spec=pltpu.PrefetchScalarGridSpec(
            num_scalar_prefetch=1,
            grid=(n_seq, steps),
            in_specs=in_specs,
            out_specs=pl.BlockSpec((nq, w), lambda b, g, pt: (b, 0)),
            scratch_shapes=scratch),
        compiler_params=_params("parallel", "arbitrary"),
        name="sample_attn_" + kind,
    )(page_table, *args)


def _dec_stickbreak(proj, k_cache, v_cache, page_table, li, nq):
    n_seq, n_pages = page_table.shape
    page = k_cache.shape[2] // N_HEADS
    n_head = 8 if n_pages >= 16 else n_pages // 2
    n_pg = _pick(n_pages - n_head, (8, 4, 2, 1))
    steps = (n_pages - n_head) // n_pg
    rows, w = N_HEADS * nq, GROUP_W
    assert 0 < n_head < n_pages
    page_block = (None, None, page * N_HEADS, DH)
    qh = pltpu.VMEM((N_HEADS, nq, DH), BF16)
    state = [pltpu.VMEM((rows, 1), F32), pltpu.VMEM((rows, DH), F32), pltpu.SMEM((1,), F32)]

    def head_page(t):
        return pl.BlockSpec(page_block, lambda b, pt: (li, pt[b, n_pages - 1 - t], 0, 0))

    row_spec = lambda col: pl.BlockSpec((nq, w), lambda b, pt: (b, col))
    acc0, r0 = pl.pallas_call(
        functools.partial(_dec_d_head_kernel, n_pg=n_head, nq=nq, page=page),
        out_shape=[jax.ShapeDtypeStruct((n_seq * rows, DH), F32),
                   jax.ShapeDtypeStruct((n_seq * rows, V7X_LANES), F32)],
        grid_spec=pltpu.PrefetchScalarGridSpec(
            num_scalar_prefetch=1,
            grid=(n_seq,),
            in_specs=[row_spec(COL_QD), row_spec(COL_KD), row_spec(COL_VD)]
            + [head_page(t) for t in range(n_head)] * 2,
            out_specs=[pl.BlockSpec((rows, DH), lambda b, pt: (b, 0)),
                       pl.BlockSpec((rows, V7X_LANES), lambda b, pt: (b, 0))],
            scratch_shapes=[qh, pltpu.VMEM((page, w), F32), pltpu.VMEM((page, w), F32)] + state),
        compiler_params=_params("parallel"),
        name="sample_stickbreak_head",
    )(page_table, proj, proj, proj, *([k_cache] * n_head), *([v_cache] * n_head))

    dead = (jnp.max(r0.reshape(n_seq, -1), axis=1) < SB_DEAD).astype(jnp.int32)

    def tail_page(t):
        def imap(b, g, pt, dd):
            idx = n_pages - 1 - n_head - (g * n_pg + t)
            return (li, jnp.where(dd[b] == 1, 0, pt[b, idx]), 0, 0)
        return pl.BlockSpec(page_block, imap)

    return pl.pallas_call(
        functools.partial(_dec_d_tail_kernel, n_pg=n_pg, nq=nq, page=page),
        out_shape=jax.ShapeDtypeStruct((n_seq * nq, w), F32),
        grid_spec=pltpu.PrefetchScalarGridSpec(
            num_scalar_prefetch=2,
            grid=(n_seq, steps),
            in_specs=[pl.BlockSpec((nq, w), lambda b, g, pt, dd: (b, COL_QD)),
                      pl.BlockSpec((rows, DH), lambda b, g, pt, dd: (b, 0)),
                      pl.BlockSpec((rows, V7X_LANES), lambda b, g, pt, dd: (b, 0))]
            + [tail_page(t) for t in range(n_pg)] * 2,
            out_specs=pl.BlockSpec((nq, w), lambda b, g, pt, dd: (b, 0)),
            scratch_shapes=[qh] + state),
        compiler_params=_params("parallel", "arbitrary"),
        name="sample_stickbreak_tail",
    )(page_table, dead, proj, acc0, r0, *([k_cache] * n_pg), *([v_cache] * n_pg))


KV_COLS = (COL_KA, COL_VA, COL_KB, COL_VB, COL_KD, COL_VD)


def _new_rows_kernel(*refs, n_layers, tm):
    n_in = n_layers * len(KV_COLS)
    ins, outs = refs[:n_in], refs[n_in:]
    for li in range(n_layers):
        @pl.when(pl.program_id(0) == li)
        def _():
            for gi, o_ref in enumerate(outs):
                src = ins[li * len(KV_COLS) + gi]
                for h in range(N_HEADS):
                    o_ref[pl.ds(h, tm, stride=N_HEADS), :] = src[:, h * DH:(h + 1) * DH]


def _new_rows(projs):
    n_layers = len(projs)
    t = projs[0].shape[0]
    tm = _pick(t, _SMALL_ROW_TILES)
    nt = t // tm

    def in_spec(li, col):
        def imap(d, i):
            return (jnp.where(d == li, i, jnp.where(d > li, nt - 1, 0)), col)
        return pl.BlockSpec((tm, GROUP_W), imap)

    in_specs = [in_spec(li, col) for li in range(n_layers) for col in KV_COLS]
    args = [projs[li] for li in range(n_layers) for _ in KV_COLS]
    out_shape = [jax.ShapeDtypeStruct((n_layers, t * N_HEADS, DH), F32)] * len(KV_COLS)
    out_specs = [pl.BlockSpec((None, tm * N_HEADS, DH), lambda d, i: (d, i, 0))] * len(KV_COLS)
    return pl.pallas_call(
        functools.partial(_new_rows_kernel, n_layers=n_layers, tm=tm),
        out_shape=out_shape,
        grid=(n_layers, nt),
        in_specs=in_specs,
        out_specs=out_specs,
        compiler_params=_params("arbitrary", "arbitrary"),
        name="new_kv_rows",
    )(*args)


def kernel(x_prompt, x_sample, cache_a_k, cache_a_v, cache_b_k, cache_b_v, cache_d_k, cache_d_v, state_c_conv, page_table, g_mix, w_in, lam_a, g_a_sub, b_c_glu, w_c_dw, b_c_dw, g_c_norm, b_c_norm, w_out, g_ffn, w_gate, w_up, w_down, g_final):
    n_b, s, d = x_prompt.shape
    n_seq, nq, _ = x_sample.shape
    depth = w_in.shape[0]
    assert n_b == 1, "prompt kernels keep one sequence's keys resident"

    def paged(c):
        assert c.shape[3:] == (N_HEADS, DH)
        return c.reshape(c.shape[0], c.shape[1], c.shape[2] * N_HEADS, DH)

    caches = [paged(c) for c in (cache_a_k, cache_a_v, cache_b_k, cache_b_v, cache_d_k, cache_d_v)]
    row = lambda v: v.reshape(1, -1)
    zero_buf = jnp.zeros((n_b, CONV_W - 1, GROUP_W), F32)

    xp = x_prompt.reshape(n_b * s, d)
    xs = x_sample.reshape(n_seq * nq, d)
    projs_p, projs_s, bufs_p, bufs_s = [], [], [], []

    for li in range(depth):
        w_in_l, w_out_l = w_in[li].astype(BF16), w_out[li].astype(BF16)
        wg_l, wu_l, wd_l = w_gate[li].astype(BF16), w_up[li].astype(BF16), w_down[li].astype(BF16)
        lam_init = 0.8 - 0.6 * math.exp(-0.3 * li)
        conv_w = (row(b_c_glu[li]), w_c_dw[li], row(b_c_dw[li]), row(g_c_norm[li]), row(b_c_norm[li]))
        last = li == depth - 1

        proj, proj_b = _norm_matmul(xp, row(g_mix[li]), w_in_l, with_bf16=True)
        o_a = _attn_a(proj, proj_b, s, lam_a[li], row(g_a_sub[li]), lam_init)
        o_b = _attn_b(proj, proj_b, s)
        o_c, buf_p = _conv_group(proj, n_b, s, zero_buf, *conv_w, out_dtype=BF16)
        o_d = _attn_d(proj, proj_b, s)
        xp = _out_proj(xp, (o_a, o_b, o_c, o_d), w_out_l)
        xp = _ffn(xp, row(g_ffn[li]), wg_l, wu_l, wd_l, row(g_final), final_norm=last)
        projs_p.append(proj)
        bufs_p.append(buf_p)

        (proj,) = _norm_matmul(xs, row(g_mix[li]), w_in_l, with_bf16=False)
        o_a = _dec_attention("a", proj, caches[0], caches[1], page_table, li, nq,
                             (COL_QA, COL_KA, COL_VA), extra=(lam_a[li], row(g_a_sub[li])),
                             lam_init=lam_init)
        o_b = _dec_attention("b", proj, caches[2], caches[3], page_table, li, nq,
                             (COL_QB, COL_KB, COL_VB))
        o_c, buf_s = _conv_group(proj, n_seq, nq, state_c_conv[li], *conv_w, out_dtype=F32)
        o_d = _dec_stickbreak(proj, caches[4], caches[5], page_table, li, nq)
        xs = _out_proj(xs, (o_a, o_b, o_c, o_d), w_out_l)
        xs = _ffn(xs, row(g_ffn[li]), wg_l, wu_l, wd_l, row(g_final), final_norm=last)
        projs_s.append(proj)
        bufs_s.append(buf_s)

    outs_p = [o.reshape(depth, n_b, s, N_HEADS, DH) for o in _new_rows(projs_p)]
    outs_s = [o.reshape(depth, n_seq, nq, N_HEADS, DH) for o in _new_rows(projs_s)]
    return (xp.reshape(n_b, s, d), xs.reshape(n_seq, nq, d),
            *outs_p, jnp.stack(bufs_p, axis=0), *outs_s, jnp.stack(bufs_s, axis=0))
```

```python
import functools
import math

import jax
import jax.numpy as jnp
from jax import lax
from jax.experimental import pallas as pl
from jax.experimental.pallas import tpu as pltpu

F32 = jnp.float32
BF16 = jnp.bfloat16

EPS = 1e-6
N_HEADS = 4
DH = 128
GROUP_W = N_HEADS * DH
DH_A = DH // 2
MOBA_BLOCK = 256
MOBA_TOPK = 3
CONV_W = 31
CONV_HALO = 32
N_NORM_GROUPS = 4
NEG_INF = float("-inf")
M_INIT = -1e30
MASK_BIAS = -2.0 ** 100
LOG2E = math.log2(math.e)
SB_DEAD = -150.0
V7X_LANES = 128
V7X_SUBLANES = 8
VMEM_LIMIT_BYTES = 56 * 1024 * 1024

COL_QA, COL_KA, COL_VA, COL_QB, COL_KB, COL_VB, COL_GLU_A, COL_GLU_G, COL_QD, COL_KD, COL_VD = range(11)

_NT = (((1,), (1,)), ((), ()))
_SMALL_ROW_TILES = (512, 256, 128, 64, 32, 16, 8)


def _dot(a, b):
    return jnp.dot(a, b, preferred_element_type=F32)


def _dot_nt(a, b):
    return lax.dot_general(a, b, _NT, preferred_element_type=F32)


def _rms(x, g):
    return x * lax.rsqrt(jnp.mean(x * x, axis=-1, keepdims=True) + EPS) * g


def _shift_div(x, n):
    assert n & (n - 1) == 0
    return lax.shift_right_logical(x, n.bit_length() - 1)


def _pick(n, candidates):
    for c in candidates:
        if n % c == 0:
            return c
    raise ValueError(f"no tile for {n} in {candidates}")


def _params(*sem):
    return pltpu.CompilerParams(dimension_semantics=sem, vmem_limit_bytes=VMEM_LIMIT_BYTES)


def _norm_matmul_kernel(x_ref, g_ref, w_ref, o_ref, *rest):
    h_ref = rest[-1]

    @pl.when(pl.program_id(1) == 0)
    def _():
        h_ref[...] = _rms(x_ref[...], g_ref[...]).astype(BF16)

    y = _dot(h_ref[...], w_ref[...])
    o_ref[...] = y
    if len(rest) == 2:
        rest[0][...] = y.astype(BF16)


def _norm_matmul(x, g, w, with_bf16):
    t, d = x.shape
    n = w.shape[1]
    tm = _pick(t, (1024,) + _SMALL_ROW_TILES)
    tn = _pick(n, (1408, 512, 256, 128))
    out_spec = pl.BlockSpec((tm, tn), lambda i, j: (i, j))
    out_shape = [jax.ShapeDtypeStruct((t, n), F32)]
    if with_bf16:
        out_shape.append(jax.ShapeDtypeStruct((t, n), BF16))
    return pl.pallas_call(
        _norm_matmul_kernel,
        out_shape=out_shape,
        grid=(t // tm, n // tn),
        in_specs=[pl.BlockSpec((tm, d), lambda i, j: (i, 0)),
                  pl.BlockSpec((1, d), lambda i, j: (0, 0)),
                  pl.BlockSpec((d, tn), lambda i, j: (0, j))],
        out_specs=[out_spec] * len(out_shape),
        scratch_shapes=[pltpu.VMEM((tm, d), BF16)],
        compiler_params=_params("parallel", "arbitrary"),
        name="norm_in_proj",
    )(x, g, w)


def _out_proj_kernel(x_ref, a_ref, b_ref, c_ref, d_ref, w_ref, o_ref):
    acc = x_ref[...]
    for gi, m_ref in enumerate((a_ref, b_ref, c_ref, d_ref)):
        acc = acc + _dot(m_ref[...].astype(BF16), w_ref[gi * GROUP_W:(gi + 1) * GROUP_W, :])
    o_ref[...] = acc


def _out_proj(x, mixes, w):
    t, d = x.shape
    tm = _pick(t, (1024,) + _SMALL_ROW_TILES)
    mix_spec = pl.BlockSpec((tm, GROUP_W), lambda i: (i, 0))
    return pl.pallas_call(
        _out_proj_kernel,
        out_shape=jax.ShapeDtypeStruct((t, d), F32),
        grid=(t // tm,),
        in_specs=[pl.BlockSpec((tm, d), lambda i: (i, 0)), mix_spec, mix_spec, mix_spec, mix_spec,
                  pl.BlockSpec(w.shape, lambda i: (0, 0))],
        out_specs=pl.BlockSpec((tm, d), lambda i: (i, 0)),
        compiler_params=_params("parallel"),
        name="out_proj",
    )(x, *mixes, w)


def _ffn_kernel(x_ref, g_ref, wg_ref, wu_ref, wd_ref, gf_ref, o_ref, h_ref, acc_ref, *, final_norm):
    j = pl.program_id(1)

    @pl.when(j == 0)
    def _():
        h_ref[...] = _rms(x_ref[...], g_ref[...]).astype(BF16)
        acc_ref[...] = jnp.zeros_like(acc_ref)

    h = h_ref[...]
    gate = _dot(h, wg_ref[...])
    up = _dot(h, wu_ref[...])
    act = (gate * jax.nn.sigmoid(gate) * up).astype(BF16)
    acc_ref[...] += _dot(act, wd_ref[...])

    @pl.when(j == pl.num_programs(1) - 1)
    def _():
        y = x_ref[...] + acc_ref[...]
        if final_norm:
            y = _rms(y, gf_ref[...])
        o_ref[...] = y


def _ffn(x, g, wg, wu, wd, g_final, final_norm):
    t, d = x.shape
    f = wg.shape[1]
    tm = _pick(t, _SMALL_ROW_TILES)
    tf = _pick(f, (512, 256, 128))
    return pl.pallas_call(
        functools.partial(_ffn_kernel, final_norm=final_norm),
        out_shape=jax.ShapeDtypeStruct((t, d), F32),
        grid=(t // tm, f // tf),
        in_specs=[pl.BlockSpec((tm, d), lambda i, j: (i, 0)),
                  pl.BlockSpec((1, d), lambda i, j: (0, 0)),
                  pl.BlockSpec((d, tf), lambda i, j: (0, j)),
                  pl.BlockSpec((d, tf), lambda i, j: (0, j)),
                  pl.BlockSpec((tf, d), lambda i, j: (j, 0)),
                  pl.BlockSpec((1, d), lambda i, j: (0, 0))],
        out_specs=pl.BlockSpec((tm, d), lambda i, j: (i, 0)),
        scratch_shapes=[pltpu.VMEM((tm, d), BF16), pltpu.VMEM((tm, d), F32)],
        compiler_params=_params("parallel", "arbitrary"),
        name="ffn",
    )(x, g, wg, wu, wd, g_final)


def _lambda(lam_ref, lam_init):
    lp = lam_ref[...]
    s01 = jnp.sum(lp[0:1, :] * lp[1:2, :], axis=1, keepdims=True)
    s23 = jnp.sum(lp[2:3, :] * lp[3:4, :], axis=1, keepdims=True)
    return jnp.exp(s01) - jnp.exp(s23) + lam_init


def _split_maps(q):
    lane = lax.broadcasted_iota(jnp.int32, q.shape, 1)
    return jnp.concatenate([jnp.where(lane < DH_A, q, 0.0), jnp.where(lane >= DH_A, q, 0.0)], axis=0)


def _softmax_update(s, m_ref, l_ref):
    m_prev = m_ref[...]
    m_new = jnp.maximum(m_prev, jnp.max(s, axis=1, keepdims=True))
    alpha = jnp.exp2(m_prev - m_new)
    p = jnp.exp2(s - m_new)
    l_ref[...] = alpha * l_ref[...] + jnp.sum(p, axis=1, keepdims=True)
    m_ref[...] = m_new
    return p, alpha


def _flash_loop(segments, n_total, scores_fn, values_fn, m_ref, acc_ref, s_ref, p_ref, a_ref):
    reps = s_ref.shape[1] // V7X_LANES
    m_ref[...] = jnp.full_like(m_ref, M_INIT)
    acc_ref[...] = jnp.zeros_like(acc_ref)
    p_ref[...] = jnp.zeros_like(p_ref)
    a_ref[...] = jnp.ones_like(a_ref)
    s_ref[...] = scores_fn(0)

    def finish(j):
        acc_ref[:, :DH] = a_ref[...] * acc_ref[:, :DH] + _dot(p_ref[...], values_fn(j))

    def make_body(keep_fn):
        def body(j, carry):
            finish(jnp.maximum(j - 1, 0))
            s = s_ref[...]
            s_ref[...] = scores_fn(jnp.minimum(j + 1, n_total - 1))
            if keep_fn is not None:
                s = jnp.where(keep_fn(j), s, NEG_INF)
            m_prev = m_ref[...]
            m_new = jnp.maximum(m_prev, jnp.max(s, axis=1, keepdims=True))
            alpha = jnp.exp2(m_prev - m_new)
            p = jnp.exp2(s - jnp.concatenate([m_new] * reps, axis=1))
            acc_ref[:, DH:] = alpha * acc_ref[:, DH:] + jnp.sum(p, axis=1, keepdims=True)
            a_ref[...] = alpha
            m_ref[...] = m_new
            p_ref[...] = p.astype(BF16)
            return carry
        return body

    for lo, hi, keep_fn in segments:
        lax.fori_loop(lo, hi, make_body(keep_fn), 0)
    finish(n_total - 1)
    acc = acc_ref[...]
    return acc[:, :DH], acc[:, DH:]


def _flash_scratch(rows, tk):
    return [pltpu.VMEM((rows, V7X_LANES), F32), pltpu.VMEM((rows, 2 * DH), F32),
            pltpu.VMEM((rows, tk), F32), pltpu.VMEM((rows, tk), BF16),
            pltpu.VMEM((rows, V7X_LANES), F32)]


def _softplus(z):
    return jnp.maximum(z, 0.0) + jnp.log(1.0 + jnp.exp(-jnp.abs(z)))


def _later_sum(lk, upper):
    hi = lk.astype(BF16)
    lo = (lk - hi.astype(F32)).astype(BF16)
    return _dot(hi, upper) + _dot(lo, upper)


def _suffix_matrix(n):
    j = lax.broadcasted_iota(jnp.int32, (n, n), 0)
    s = lax.broadcasted_iota(jnp.int32, (n, n), 1)
    return (j > s).astype(BF16)


def _attn_a_kernel(lam_ref, gsub_ref, q_ref, k_ref, v_ref, o_ref, *scratch, tq, tk, lam_init):
    qi = pl.program_id(1)
    qq = _split_maps(q_ref[...] * (DH_A ** -0.5 * LOG2E)).astype(BF16)

    def block(ref, j):
        return ref[pl.ds(pl.multiple_of(j * tk, tk), tk), :]

    def causal(j):
        row = lax.broadcasted_iota(jnp.int32, (2 * tq, tk), 0)
        col = lax.broadcasted_iota(jnp.int32, (2 * tq, tk), 1)
        q_pos = qi * tq + jnp.where(row >= tq, row - tq, row)
        return j * tk + col <= q_pos

    n_full = (qi * tq) // tk
    n_total = (qi * tq + tq - 1) // tk + 1
    acc, l = _flash_loop([(0, n_full, None), (n_full, n_total, causal)], n_total,
                         lambda j: _dot_nt(qq, block(k_ref, j)), lambda j: block(v_ref, j), *scratch)
    lam = _lambda(lam_ref, lam_init)
    o = acc[:tq] / l[:tq] - lam * (acc[tq:] / l[tq:])
    o_ref[...] = (_rms(o, gsub_ref[...]) * (1.0 - lam_init)).astype(o_ref.dtype)


def _attn_a(proj, proj_b, s, lam_a, g_sub, lam_init):
    tq = _pick(s, (256, 128))
    tk = _pick(s, (512, 256, 128))
    return pl.pallas_call(
        functools.partial(_attn_a_kernel, tq=tq, tk=tk, lam_init=lam_init),
        out_shape=jax.ShapeDtypeStruct((s, GROUP_W), BF16),
        grid=(N_HEADS, s // tq),
        in_specs=[pl.BlockSpec(lam_a.shape, lambda h, i: (0, 0)),
                  pl.BlockSpec((1, DH), lambda h, i: (0, 0)),
                  pl.BlockSpec((tq, DH), lambda h, i: (i, COL_QA * N_HEADS + h)),
                  pl.BlockSpec((s, DH), lambda h, i: (0, COL_KA * N_HEADS + h)),
                  pl.BlockSpec((s, DH), lambda h, i: (0, COL_VA * N_HEADS + h))],
        out_specs=pl.BlockSpec((tq, DH), lambda h, i: (i, h)),
        scratch_shapes=_flash_scratch(2 * tq, tk),
        compiler_params=_params("parallel", "arbitrary"),
        name="prompt_diff_attn",
    )(lam_a, g_sub, proj, proj_b, proj_b)


def _block_mean_kernel(k_ref, o_ref, *, n_blocks):
    o_ref[...] = jnp.zeros_like(o_ref)

    def body(b, carry):
        rows = k_ref[pl.ds(pl.multiple_of(b * MOBA_BLOCK, MOBA_BLOCK), MOBA_BLOCK), :]
        o_ref[pl.ds(b, 1), :] = jnp.mean(rows, axis=0, keepdims=True)
        return carry

    lax.fori_loop(0, n_blocks, body, 0)


def _block_means(proj, s, n_pad):
    return pl.pallas_call(
        functools.partial(_block_mean_kernel, n_blocks=s // MOBA_BLOCK),
        out_shape=jax.ShapeDtypeStruct((N_HEADS, n_pad, DH), F32),
        grid=(N_HEADS,),
        in_specs=[pl.BlockSpec((s, DH), lambda h: (0, COL_KB * N_HEADS + h))],
        out_specs=pl.BlockSpec((None, n_pad, DH), lambda h: (h, 0, 0)),
        compiler_params=_params("parallel"),
        name="moba_block_means",
    )(proj)


def _moba_select(gates, own):
    lane = lax.broadcasted_iota(jnp.int32, gates.shape, 1)
    g = jnp.where(lane < own, gates, NEG_INF)
    sel = jnp.where(lane == own, 1.0, 0.0)
    for _ in range(MOBA_TOPK):
        mx = jnp.max(g, axis=1, keepdims=True)
        pick = jnp.min(jnp.where(g == mx, lane, gates.shape[1]), axis=1, keepdims=True)
        hit = jnp.logical_and(lane == pick, mx > NEG_INF)
        sel = jnp.where(hit, 1.0, sel)
        g = jnp.where(lane == pick, NEG_INF, g)
    return sel


def _attn_b_kernel(q_ref, kmean_ref, k_ref, v_ref, o_ref, *scratch, t):
    qi = pl.program_id(1)
    per_tile = t // MOBA_BLOCK
    q_raw = q_ref[...]
    gates = lax.dot_general(q_raw, kmean_ref[...], _NT, precision=lax.Precision.HIGHEST,
                            preferred_element_type=F32)
    row1 = lax.broadcasted_iota(jnp.int32, (t, 1), 0)
    own = qi * per_tile + _shift_div(row1, MOBA_BLOCK)
    sel = _moba_select(gates, own)
    q = jnp.concatenate([(q_raw * (DH ** -0.5 * LOG2E)).astype(BF16),
                         jnp.where(sel > 0.5, 0.0, MASK_BIAS).astype(BF16)], axis=1)

    def block(ref, j):
        return ref[pl.ds(pl.multiple_of(j * t, t), t), :]

    def scores(j):
        key = lax.broadcasted_iota(jnp.int32, (t, sel.shape[1]), 0)
        blk = lax.broadcasted_iota(jnp.int32, (t, sel.shape[1]), 1)
        onehot = (blk == j * per_tile + _shift_div(key, MOBA_BLOCK)).astype(BF16)
        return _dot_nt(q, jnp.concatenate([block(k_ref, j), onehot], axis=1))

    def causal(j):
        row = lax.broadcasted_iota(jnp.int32, (t, t), 0)
        col = lax.broadcasted_iota(jnp.int32, (t, t), 1)
        return col <= row

    acc, l = _flash_loop([(0, qi, None), (qi, qi + 1, causal)], qi + 1,
                         scores, lambda j: block(v_ref, j), *scratch)
    o_ref[...] = (acc / l).astype(o_ref.dtype)


def _attn_b(proj, proj_b, s):
    t = _pick(s, (512, 256))
    n_pad = -(-(s // MOBA_BLOCK) // V7X_LANES) * V7X_LANES
    kmean = _block_means(proj, s, n_pad)
    return pl.pallas_call(
        functools.partial(_attn_b_kernel, t=t),
        out_shape=jax.ShapeDtypeStruct((s, GROUP_W), BF16),
        grid=(N_HEADS, s // t),
        in_specs=[pl.BlockSpec((t, DH), lambda h, i: (i, COL_QB * N_HEADS + h)),
                  pl.BlockSpec((None, n_pad, DH), lambda h, i: (h, 0, 0)),
                  pl.BlockSpec((s, DH), lambda h, i: (0, COL_KB * N_HEADS + h)),
                  pl.BlockSpec((s, DH), lambda h, i: (0, COL_VB * N_HEADS + h))],
        out_specs=pl.BlockSpec((t, DH), lambda h, i: (i, h)),
        scratch_shapes=_flash_scratch(t, t),
        compiler_params=_params("parallel", "arbitrary"),
        name="prompt_moba",
    )(proj, kmean, proj_b, proj_b)


def _attn_d_kernel(q_ref, k_ref, v_ref, o_ref, r_ref, acc_ref, *, tq, tk):
    qi = pl.program_id(1)
    q = (q_ref[...] * (DH ** -0.5)).astype(BF16)
    r_ref[...] = jnp.zeros_like(r_ref)
    acc_ref[...] = jnp.zeros_like(acc_ref)

    def cond(carry):
        kb, r_max = carry
        return jnp.logical_and(kb >= 0, r_max > SB_DEAD)

    def body(carry):
        kb, _ = carry
        start = pl.multiple_of(kb * tk, tk)
        z = _dot_nt(q, k_ref[pl.ds(start, tk), :])
        row = lax.broadcasted_iota(jnp.int32, (tq, tk), 0)
        col = lax.broadcasted_iota(jnp.int32, (tq, tk), 1)
        strict = start + col < qi * tq + row
        sp = _softplus(z)
        lk = jnp.where(strict, -sp, 0.0)
        later = _later_sum(lk, _suffix_matrix(tk))
        r = r_ref[...]
        w = jnp.where(strict, jnp.exp(z - sp + later + r), 0.0)
        acc_ref[...] += _dot(w.astype(BF16), v_ref[pl.ds(start, tk), :])
        r_new = r + jnp.sum(lk, axis=1, keepdims=True)
        r_ref[...] = r_new
        return kb - 1, jnp.max(r_new)

    lax.while_loop(cond, body, ((qi * tq + tq - 1) // tk, jnp.float32(0.0)))
    o_ref[...] = acc_ref[...].astype(o_ref.dtype)


def _attn_d(proj, proj_b, s):
    tq = _pick(s, (256, 128))
    tk = _pick(s, (256, 128))
    return pl.pallas_call(
        functools.partial(_attn_d_kernel, tq=tq, tk=tk),
        out_shape=jax.ShapeDtypeStruct((s, GROUP_W), BF16),
        grid=(N_HEADS, s // tq),
        in_specs=[pl.BlockSpec((tq, DH), lambda h, i: (i, COL_QD * N_HEADS + h)),
                  pl.BlockSpec((s, DH), lambda h, i: (0, COL_KD * N_HEADS + h)),
                  pl.BlockSpec((s, DH), lambda h, i: (0, COL_VD * N_HEADS + h))],
        out_specs=pl.BlockSpec((tq, DH), lambda h, i: (i, h)),
        scratch_shapes=[pltpu.VMEM((tq, 1), F32), pltpu.VMEM((tq, DH), F32)],
        compiler_params=_params("parallel", "arbitrary"),
        name="prompt_stickbreak",
    )(proj, proj_b, proj_b)


def _conv_kernel(*refs, tt, has_prev):
    if has_prev:
        cur_ref, prev_ref, cbuf_ref, bglu_ref, wdw_ref, bdw_ref, gn_ref, bn_ref, o_ref, nbuf_ref, ubuf_ref = refs
    else:
        cur_ref, cbuf_ref, bglu_ref, wdw_ref, bdw_ref, gn_ref, bn_ref, o_ref, nbuf_ref, ubuf_ref = refs
    i = pl.program_id(1)
    c = GROUP_W

    def glu(x):
        x = x + bglu_ref[...]
        return x[:, :c] * jax.nn.sigmoid(x[:, c:])

    ubuf_ref[CONV_HALO:CONV_HALO + tt, :] = glu(cur_ref[...])

    @pl.when(i == 0)
    def _():
        ubuf_ref[CONV_HALO - (CONV_W - 1):CONV_HALO, :] = cbuf_ref[...]

    if has_prev:
        @pl.when(i > 0)
        def _():
            ubuf_ref[0:CONV_HALO, :] = glu(prev_ref[...])

    off = CONV_HALO - (CONV_W - 1)
    y = jnp.zeros((tt, c), F32) + bdw_ref[...]
    for j in range(CONV_W):
        y = y + ubuf_ref[off + j:off + j + tt, :] * wdw_ref[j:j + 1, :]

    gw = c // N_NORM_GROUPS
    for gi in range(N_NORM_GROUPS):
        seg = y[:, gi * gw:(gi + 1) * gw]
        mu = jnp.mean(seg, axis=-1, keepdims=True)
        var = jnp.mean(jnp.square(seg - mu), axis=-1, keepdims=True)
        yn = (seg - mu) * lax.rsqrt(var + EPS)
        yn = yn * gn_ref[:, gi * gw:(gi + 1) * gw] + bn_ref[:, gi * gw:(gi + 1) * gw]
        o_ref[:, gi * gw:(gi + 1) * gw] = (yn * jax.nn.sigmoid(yn)).astype(o_ref.dtype)

    @pl.when(i == pl.num_programs(1) - 1)
    def _():
        nbuf_ref[...] = ubuf_ref[CONV_HALO + tt - (CONV_W - 1):CONV_HALO + tt, :]


def _conv_group(proj, batch, s, conv_buf, b_glu, w_dw, b_dw, g_norm, b_norm, out_dtype):
    tt = _pick(s, _SMALL_ROW_TILES)
    nt = s // tt
    has_prev = nt > 1
    assert tt + CONV_HALO >= CONV_W - 1 and (not has_prev or tt % CONV_HALO == 0)
    c = GROUP_W
    glu_col = COL_GLU_A // 2
    in_specs = [pl.BlockSpec((tt, 2 * c), lambda b, i: (b * nt + i, glu_col))]
    args = [proj]
    if has_prev:
        per = tt // CONV_HALO
        in_specs.append(pl.BlockSpec((CONV_HALO, 2 * c),
                                     lambda b, i: (jnp.maximum((b * nt + i) * per - 1, 0), glu_col)))
        args.append(proj)
    vec = lambda n: pl.BlockSpec((1, n), lambda b, i: (0, 0))
    in_specs += [pl.BlockSpec((None, CONV_W - 1, c), lambda b, i: (b, 0, 0)),
                 vec(2 * c), pl.BlockSpec((CONV_W, c), lambda b, i: (0, 0)), vec(c), vec(c), vec(c)]
    args += [conv_buf, b_glu, w_dw, b_dw, g_norm, b_norm]
    return pl.pallas_call(
        functools.partial(_conv_kernel, tt=tt, has_prev=has_prev),
        out_shape=(jax.ShapeDtypeStruct((batch * s, c), out_dtype),
                   jax.ShapeDtypeStruct((batch, CONV_W - 1, c), F32)),
        grid=(batch, nt),
        in_specs=in_specs,
        out_specs=(pl.BlockSpec((tt, c), lambda b, i: (b * nt + i, 0)),
                   pl.BlockSpec((None, CONV_W - 1, c), lambda b, i: (b, 0, 0))),
        scratch_shapes=[pltpu.VMEM((CONV_HALO + tt, c), F32)],
        compiler_params=_params("parallel", "arbitrary"),
        name="conv_module",
    )(*args)


def _page_heads(ref, page, dtype):
    return [ref[pl.ds(h, page, stride=N_HEADS), :].astype(dtype) for h in range(N_HEADS)]


def _pad_heads(pad_ref, new_ref, dtype):
    pad_ref[...] = jnp.zeros_like(pad_ref)
    pad_ref[0:new_ref.shape[0], :] = new_ref[...]
    return [pad_ref[:, h * DH:(h + 1) * DH].astype(dtype) for h in range(N_HEADS)]


def _head_rows(fn):
    return jnp.concatenate([fn(h) for h in range(N_HEADS)], axis=0)


def _new_key_index(shape, nq):
    row = lax.broadcasted_iota(jnp.int32, shape, 0)
    col = lax.broadcasted_iota(jnp.int32, shape, 1)
    return col, row & (nq - 1)


def _dec_a_kernel(pt_ref, lam_ref, gsub_ref, q_ref, kn_ref, vn_ref, *rest, n_pg, nq, page, lam_init):
    del pt_ref
    k_pages, v_pages = rest[:n_pg], rest[n_pg:2 * n_pg]
    o_ref, qh_ref, kpad_ref, vpad_ref, m_ref, l_ref, acc_ref = rest[2 * n_pg:]
    g = pl.program_id(1)
    rh = 2 * nq

    def reduce(s, pv_fn):
        p, alpha = _softmax_update(s, m_ref, l_ref)
        pv = _head_rows(lambda h: pv_fn(h, p[h * rh:(h + 1) * rh].astype(BF16)))
        acc_ref[...] = alpha * acc_ref[...] + pv

    @pl.when(g == 0)
    def _():
        for h in range(N_HEADS):
            qh = q_ref[:, h * DH:(h + 1) * DH] * (DH_A ** -0.5 * LOG2E)
            qh_ref[h] = _split_maps(qh).astype(BF16)
        m_ref[...] = jnp.full_like(m_ref, M_INIT)
        l_ref[...] = jnp.zeros_like(l_ref)
        acc_ref[...] = jnp.zeros_like(acc_ref)
        kh = _pad_heads(kpad_ref, kn_ref, BF16)
        vh = _pad_heads(vpad_ref, vn_ref, BF16)
        s = _head_rows(lambda h: _dot_nt(qh_ref[h], kh[h]))
        key, qry = _new_key_index(s.shape, nq)
        reduce(jnp.where(key <= qry, s, NEG_INF), lambda h, ph: _dot(ph, vh[h]))

    kh = [_page_heads(r, page, BF16) for r in k_pages]
    vh = [_page_heads(r, page, BF16) for r in v_pages]
    s = jnp.concatenate([_head_rows(lambda h: _dot_nt(qh_ref[h], kh[t][h])) for t in range(n_pg)],
                        axis=1)
    reduce(s, lambda h, ph: sum(_dot(ph[:, t * page:(t + 1) * page], vh[t][h]) for t in range(n_pg)))

    @pl.when(g == pl.num_programs(1) - 1)
    def _():
        acc = acc_ref[...]
        l = l_ref[...]
        lam = _lambda(lam_ref, lam_init)
        for h in range(N_HEADS):
            r1, r2 = h * rh, h * rh + nq
            o = acc[r1:r1 + nq] / l[r1:r1 + nq] - lam * (acc[r2:r2 + nq] / l[r2:r2 + nq])
            o_ref[:, h * DH:(h + 1) * DH] = _rms(o, gsub_ref[...]) * (1.0 - lam_init)


def _stickbreak_pages(qh_ref, k_pages, v_pages, r_ref, acc_ref, rmax_ref, nq, page):
    def step(kh, vh, strict, r):
        z = _head_rows(lambda h: _dot_nt(qh_ref[h], kh[h]))
        sp = _softplus(z)
        lk = -sp if strict is None else jnp.where(strict(z.shape), -sp, 0.0)
        later = _later_sum(lk, _suffix_matrix(page))
        w = jnp.exp(z - sp + later + r)
        if strict is not None:
            w = jnp.where(strict(z.shape), w, 0.0)
        acc_ref[...] += _head_rows(lambda h: _dot(w[h * nq:(h + 1) * nq].astype(BF16), vh[h]))
        r_new = r + jnp.sum(lk, axis=1, keepdims=True)
        r_ref[...] = r_new
        rmax_ref[0] = jnp.max(r_new)

    def run_pages():
        for k_ref, v_ref in zip(k_pages, v_pages):
            @pl.when(rmax_ref[0] > SB_DEAD)
            def _():
                step(_page_heads(k_ref, page, BF16), _page_heads(v_ref, page, BF16), None, r_ref[...])

    return step, run_pages


def _dec_d_head_kernel(pt_ref, q_ref, kn_ref, vn_ref, *rest, n_pg, nq, page):
    del pt_ref
    k_pages, v_pages = rest[:n_pg], rest[n_pg:2 * n_pg]
    acc_out, r_out, qh_ref, kpad_ref, vpad_ref, r_ref, acc_ref, rmax_ref = rest[2 * n_pg:]
    step, run_pages = _stickbreak_pages(qh_ref, k_pages, v_pages, r_ref, acc_ref, rmax_ref, nq, page)
    for h in range(N_HEADS):
        qh_ref[h] = (q_ref[:, h * DH:(h + 1) * DH] * (DH ** -0.5)).astype(BF16)
    acc_ref[...] = jnp.zeros_like(acc_ref)

    def strict(shape):
        key, qry = _new_key_index(shape, nq)
        return key < qry

    step(_pad_heads(kpad_ref, kn_ref, BF16), _pad_heads(vpad_ref, vn_ref, BF16), strict,
         jnp.zeros(r_ref.shape, F32))
    run_pages()
    acc_out[...] = acc_ref[...]
    r_out[...] = jnp.broadcast_to(r_ref[...], r_out.shape)


def _dec_d_tail_kernel(pt_ref, dead_ref, q_ref, acc_in, r_in, *rest, n_pg, nq, page):
    del pt_ref
    k_pages, v_pages = rest[:n_pg], rest[n_pg:2 * n_pg]
    o_ref, qh_ref, r_ref, acc_ref, rmax_ref = rest[2 * n_pg:]
    g = pl.program_id(1)
    _, run_pages = _stickbreak_pages(qh_ref, k_pages, v_pages, r_ref, acc_ref, rmax_ref, nq, page)

    @pl.when(g == 0)
    def _():
        for h in range(N_HEADS):
            qh_ref[h] = (q_ref[:, h * DH:(h + 1) * DH] * (DH ** -0.5)).astype(BF16)
        acc_ref[...] = acc_in[...]
        r = r_in[:, 0:1]
        r_ref[...] = r
        rmax_ref[0] = jnp.where(dead_ref[pl.program_id(0)] == 1, NEG_INF, jnp.max(r))

    run_pages()

    @pl.when(g == pl.num_programs(1) - 1)
    def _():
        acc = acc_ref[...]
        for h in range(N_HEADS):
            o_ref[:, h * DH:(h + 1) * DH] = acc[h * nq:(h + 1) * nq]


def _dec_b_kernel(pt_ref, q_ref, kn_ref, vn_ref, *rest, n_pg, nq, page, n_blocks):
    del pt_ref
    k_pages, v_pages = rest[:n_pg], rest[n_pg:2 * n_pg]
    o_ref, qf_ref, qh_ref, kpad_ref, vpad_ref, g_ref, m_ref, l_ref, acc_ref = rest[2 * n_pg:]
    g = pl.program_id(1)
    per_block = MOBA_BLOCK // page
    lane = lax.broadcasted_iota(jnp.int32, g_ref.shape, 1)

    @pl.when(g == 0)
    def _():
        for h in range(N_HEADS):
            qh = q_ref[:, h * DH:(h + 1) * DH]
            qf_ref[h] = qh
            qh_ref[h] = (qh * (DH ** -0.5 * LOG2E)).astype(BF16)
        g_ref[...] = jnp.full_like(g_ref, NEG_INF)
        m_ref[...] = jnp.full_like(m_ref, M_INIT)
        l_ref[...] = jnp.zeros_like(l_ref)

    gates, maxes, sums = g_ref[...], m_ref[...], l_ref[...]
    for blk in range(n_pg // per_block):
        kf = [_page_heads(k_pages[blk * per_block + t], page, F32) for t in range(per_block)]
        vh = [_page_heads(v_pages[blk * per_block + t], page, BF16) for t in range(per_block)]

        def gate_of(h):
            k_mean = sum(jnp.sum(kf[t][h], axis=0, keepdims=True) for t in range(per_block))
            return jnp.sum(qf_ref[h] * (k_mean * (1.0 / MOBA_BLOCK)), axis=1, keepdims=True)

        gate = _head_rows(gate_of)
        s = jnp.concatenate(
            [_head_rows(lambda h: _dot_nt(qh_ref[h], kf[t][h].astype(BF16))) for t in range(per_block)],
            axis=1)
        m_b = jnp.max(s, axis=1, keepdims=True)
        p = jnp.exp2(s - m_b)
        l_b = jnp.sum(p, axis=1, keepdims=True)
        acc_b = _head_rows(lambda h: sum(
            _dot(p[h * nq:(h + 1) * nq, t * page:(t + 1) * page].astype(BF16), vh[t][h])
            for t in range(per_block)))
        b_idx = g * (n_pg // per_block) + blk
        acc_ref[b_idx] = acc_b
        here = lane == b_idx
        gates = jnp.where(here, gate, gates)
        maxes = jnp.where(here, m_b, maxes)
        sums = jnp.where(here, l_b, sums)
    g_ref[...], m_ref[...], l_ref[...] = gates, maxes, sums

    @pl.when(g == pl.num_programs(1) - 1)
    def _():
        kh = _pad_heads(kpad_ref, kn_ref, BF16)
        vh = _pad_heads(vpad_ref, vn_ref, BF16)
        s = _head_rows(lambda h: _dot_nt(qh_ref[h], kh[h]))
        key, qry = _new_key_index(s.shape, nq)
        s = jnp.where(key <= qry, s, NEG_INF)
        m_o = jnp.max(s, axis=1, keepdims=True)
        p = jnp.exp2(s - m_o)
        chosen = jnp.logical_and(_moba_select(gates, n_blocks) > 0.5, lane < n_blocks)
        m_all = jnp.maximum(m_o, jnp.max(jnp.where(chosen, maxes, NEG_INF), axis=1, keepdims=True))
        weight = jnp.where(chosen, jnp.exp2(maxes - m_all), 0.0)
        w_o = jnp.exp2(m_o - m_all)
        den = w_o * jnp.sum(p, axis=1, keepdims=True) + jnp.sum(weight * sums, axis=1, keepdims=True)

        num = w_o * _head_rows(lambda h: _dot(p[h * nq:(h + 1) * nq].astype(BF16), vh[h]))
        for b in range(n_blocks):
            w_b = jnp.sum(jnp.where(lane == b, weight, 0.0), axis=1, keepdims=True)
            num = num + w_b * acc_ref[b]
        out = num / den
        for h in range(N_HEADS):
            o_ref[:, h * DH:(h + 1) * DH] = out[h * nq:(h + 1) * nq]


def _dec_attention(kind, proj, k_cache, v_cache, page_table, li, nq, cols, extra=(), **kw):
    n_seq, n_pages = page_table.shape
    page = k_cache.shape[2] // N_HEADS
    n_pg = _pick(n_pages, (16, 8, 4, 2))
    steps = n_pages // n_pg
    col_q, col_k, col_v = cols
    w = GROUP_W
    assert nq % V7X_SUBLANES == 0 and nq & (nq - 1) == 0 and nq <= page
    assert (n_pages * page) % MOBA_BLOCK == 0 and MOBA_BLOCK % page == 0

    def page_spec(t):
        return pl.BlockSpec((None, None, page * N_HEADS, DH),
                            lambda b, g, pt: (li, pt[b, g * n_pg + t], 0, 0))

    small = lambda a: pl.BlockSpec(a.shape, lambda b, g, pt: (0,) * a.ndim)
    row_spec = lambda col: pl.BlockSpec((nq, w), lambda b, g, pt: (b, col))
    in_specs = ([small(a) for a in extra] + [row_spec(col_q), row_spec(col_k), row_spec(col_v)]
                + [page_spec(t) for t in range(n_pg)] * 2)
    args = list(extra) + [proj, proj, proj] + [k_cache] * n_pg + [v_cache] * n_pg

    pad = [pltpu.VMEM((page, w), F32), pltpu.VMEM((page, w), F32)]
    stat = lambda rows: pltpu.VMEM((rows, 1), F32)
    if kind == "a":
        rows = 2 * N_HEADS * nq
        body = functools.partial(_dec_a_kernel, n_pg=n_pg, nq=nq, page=page, **kw)
        scratch = [pltpu.VMEM((N_HEADS, 2 * nq, DH), BF16)] + pad + [
            stat(rows), stat(rows), pltpu.VMEM((rows, DH), F32)]
    else:
        rows = N_HEADS * nq
        n_blocks = n_pages * page // MOBA_BLOCK
        assert n_blocks <= V7X_LANES
        body = functools.partial(_dec_b_kernel, n_pg=n_pg, nq=nq, page=page, n_blocks=n_blocks)
        lanes = lambda: pltpu.VMEM((rows, V7X_LANES), F32)
        scratch = [pltpu.VMEM((N_HEADS, nq, DH), F32), pltpu.VMEM((N_HEADS, nq, DH), BF16)] + pad + [
            lanes(), lanes(), lanes(), pltpu.VMEM((n_blocks, rows, DH), F32)]

    return pl.pallas_call(
        body,
        out_shape=jax.ShapeDtypeStruct((n_seq * nq, w), F32),
        grid_spec=pltpu.PrefetchScalarGridSpec(
            num_scalar_prefetch=1,
            grid=(n_seq, steps),
            in_specs=in_specs,
            out_specs=pl.BlockSpec((nq, w), lambda b, g, pt: (b, 0)),
            scratch_shapes=scratch),
        compiler_params=_params("parallel", "arbitrary"),
        name="sample_attn_" + kind,
    )(page_table, *args)


def _dec_stickbreak(proj, k_cache, v_cache, page_table, li, nq):
    n_seq, n_pages = page_table.shape
    page = k_cache.shape[2] // N_HEADS
    n_head = 8 if n_pages >= 16 else n_pages // 2
    n_pg = _pick(n_pages - n_head, (8, 4, 2, 1))
    steps = (n_pages - n_head) // n_pg
    rows, w = N_HEADS * nq, GROUP_W
    assert 0 < n_head < n_pages
    page_block = (None, None, page * N_HEADS, DH)
    qh = pltpu.VMEM((N_HEADS, nq, DH), BF16)
    state = [pltpu.VMEM((rows, 1), F32), pltpu.VMEM((rows, DH), F32), pltpu.SMEM((1,), F32)]

    def head_page(t):
        return pl.BlockSpec(page_block, lambda b, pt: (li, pt[b, n_pages - 1 - t], 0, 0))

    row_spec = lambda col: pl.BlockSpec((nq, w), lambda b, pt: (b, col))
    acc0, r0 = pl.pallas_call(
        functools.partial(_dec_d_head_kernel, n_pg=n_head, nq=nq, page=page),
        out_shape=[jax.ShapeDtypeStruct((n_seq * rows, DH), F32),
                   jax.ShapeDtypeStruct((n_seq * rows, V7X_LANES), F32)],
        grid_spec=pltpu.PrefetchScalarGridSpec(
            num_scalar_prefetch=1,
            grid=(n_seq,),
            in_specs=[row_spec(COL_QD), row_spec(COL_KD), row_spec(COL_VD)]
            + [head_page(t) for t in range(n_head)] * 2,
            out_specs=[pl.BlockSpec((rows, DH), lambda b, pt: (b, 0)),
                       pl.BlockSpec((rows, V7X_LANES), lambda b, pt: (b, 0))],
            scratch_shapes=[qh, pltpu.VMEM((page, w), F32), pltpu.VMEM((page, w), F32)] + state),
        compiler_params=_params("parallel"),
        name="sample_stickbreak_head",
    )(page_table, proj, proj, proj, *([k_cache] * n_head), *([v_cache] * n_head))

    dead = (jnp.max(r0.reshape(n_seq, -1), axis=1) < SB_DEAD).astype(jnp.int32)

    def tail_page(t):
        def imap(b, g, pt, dd):
            idx = n_pages - 1 - n_head - (g * n_pg + t)
            return (li, jnp.where(dd[b] == 1, 0, pt[b, idx]), 0, 0)
        return pl.BlockSpec(page_block, imap)

    tail = pl.pallas_call(
        functools.partial(_dec_d_tail_kernel, n_pg=n_pg, nq=nq, page=page),
        out_shape=jax.ShapeDtypeStruct((n_seq * nq, w), F32),
        grid_spec=pltpu.PrefetchScalarGridSpec(
            num_scalar_prefetch=2,
            grid=(n_seq, steps),
            in_specs=[pl.BlockSpec((nq, w), lambda b, g, pt, dd: (b, COL_QD)),
                      pl.BlockSpec((rows, DH), lambda b, g, pt, dd: (b, 0)),
                      pl.BlockSpec((rows, V7X_LANES), lambda b, g, pt, dd: (b, 0))]
            + [tail_page(t) for t in range(n_pg)] * 2,
            out_specs=pl.BlockSpec((nq, w), lambda b, g, pt, dd: (b, 0)),
            scratch_shapes=[qh] + state),
        compiler_params=_params("parallel", "arbitrary"),
        name="sample_stickbreak_tail",
    )

    def run_tail():
        return tail(page_table, dead, proj, acc0, r0, *([k_cache] * n_pg), *([v_cache] * n_pg))

    def all_dead():
        return acc0.reshape(n_seq, N_HEADS, nq, DH).transpose(0, 2, 1, 3).reshape(n_seq * nq, w)

    return lax.cond(jnp.any(dead == 0), run_tail, all_dead)


KV_COLS = (COL_KA, COL_VA, COL_KB, COL_VB, COL_KD, COL_VD)


def _new_rows_kernel(*refs, n_layers, tm):
    n_in = n_layers * len(KV_COLS)
    ins, outs = refs[:n_in], refs[n_in:]
    for li in range(n_layers):
        @pl.when(pl.program_id(0) == li)
        def _():
            for gi, o_ref in enumerate(outs):
                src = ins[li * len(KV_COLS) + gi]
                for h in range(N_HEADS):
                    o_ref[pl.ds(h, tm, stride=N_HEADS), :] = src[:, h * DH:(h + 1) * DH]


def _new_rows(projs):
    n_layers = len(projs)
    t = projs[0].shape[0]
    tm = _pick(t, _SMALL_ROW_TILES)
    nt = t // tm

    def in_spec(li, col):
        def imap(d, i):
            return (jnp.where(d == li, i, jnp.where(d > li, nt - 1, 0)), col)
        return pl.BlockSpec((tm, GROUP_W), imap)

    in_specs = [in_spec(li, col) for li in range(n_layers) for col in KV_COLS]
    args = [projs[li] for li in range(n_layers) for _ in KV_COLS]
    out_shape = [jax.ShapeDtypeStruct((n_layers, t * N_HEADS, DH), F32)] * len(KV_COLS)
    out_specs = [pl.BlockSpec((None, tm * N_HEADS, DH), lambda d, i: (d, i, 0))] * len(KV_COLS)
    return pl.pallas_call(
        functools.partial(_new_rows_kernel, n_layers=n_layers, tm=tm),
        out_shape=out_shape,
        grid=(n_layers, nt),
        in_specs=in_specs,
        out_specs=out_specs,
        compiler_params=_params("arbitrary", "arbitrary"),
        name="new_kv_rows",
    )(*args)


def kernel(x_prompt, x_sample, cache_a_k, cache_a_v, cache_b_k, cache_b_v, cache_d_k, cache_d_v, state_c_conv, page_table, g_mix, w_in, lam_a, g_a_sub, b_c_glu, w_c_dw, b_c_dw, g_c_norm, b_c_norm, w_out, g_ffn, w_gate, w_up, w_down, g_final):
    n_b, s, d = x_prompt.shape
    n_seq, nq, _ = x_sample.shape
    depth = w_in.shape[0]
    assert n_b == 1, "prompt kernels keep one sequence's keys resident"

    def paged(c):
        assert c.shape[3:] == (N_HEADS, DH)
        return c.reshape(c.shape[0], c.shape[1], c.shape[2] * N_HEADS, DH)

    caches = [paged(c) for c in (cache_a_k, cache_a_v, cache_b_k, cache_b_v, cache_d_k, cache_d_v)]
    row = lambda v: v.reshape(1, -1)
    zero_buf = jnp.zeros((n_b, CONV_W - 1, GROUP_W), F32)

    xp = x_prompt.reshape(n_b * s, d)
    xs = x_sample.reshape(n_seq * nq, d)
    projs_p, projs_s, bufs_p, bufs_s = [], [], [], []

    for li in range(depth):
        w_in_l, w_out_l = w_in[li].astype(BF16), w_out[li].astype(BF16)
        wg_l, wu_l, wd_l = w_gate[li].astype(BF16), w_up[li].astype(BF16), w_down[li].astype(BF16)
        lam_init = 0.8 - 0.6 * math.exp(-0.3 * li)
        conv_w = (row(b_c_glu[li]), w_c_dw[li], row(b_c_dw[li]), row(g_c_norm[li]), row(b_c_norm[li]))
        last = li == depth - 1

        proj, proj_b = _norm_matmul(xp, row(g_mix[li]), w_in_l, with_bf16=True)
        o_a = _attn_a(proj, proj_b, s, lam_a[li], row(g_a_sub[li]), lam_init)
        o_b = _attn_b(proj, proj_b, s)
        o_c, buf_p = _conv_group(proj, n_b, s, zero_buf, *conv_w, out_dtype=BF16)
        o_d = _attn_d(proj, proj_b, s)
        xp = _out_proj(xp, (o_a, o_b, o_c, o_d), w_out_l)
        xp = _ffn(xp, row(g_ffn[li]), wg_l, wu_l, wd_l, row(g_final), final_norm=last)
        projs_p.append(proj)
        bufs_p.append(buf_p)

        (proj,) = _norm_matmul(xs, row(g_mix[li]), w_in_l, with_bf16=False)
        o_a = _dec_attention("a", proj, caches[0], caches[1], page_table, li, nq,
                             (COL_QA, COL_KA, COL_VA), extra=(lam_a[li], row(g_a_sub[li])),
                             lam_init=lam_init)
        o_b = _dec_attention("b", proj, caches[2], caches[3], page_table, li, nq,
                             (COL_QB, COL_KB, COL_VB))
        o_c, buf_s = _conv_group(proj, n_seq, nq, state_c_conv[li], *conv_w, out_dtype=F32)
        o_d = _dec_stickbreak(proj, caches[4], caches[5], page_table, li, nq)
        xs = _out_proj(xs, (o_a, o_b, o_c, o_d), w_out_l)
        xs = _ffn(xs, row(g_ffn[li]), wg_l, wu_l, wd_l, row(g_final), final_norm=last)
        projs_s.append(proj)
        bufs_s.append(buf_s)

    outs_p = [o.reshape(depth, n_b, s, N_HEADS, DH) for o in _new_rows(projs_p)]
    outs_s = [o.reshape(depth, n_seq, nq, N_HEADS, DH) for o in _new_rows(projs_s)]
    return (xp.reshape(n_b, s, d), xs.reshape(n_seq, nq, d),
            *outs_p, jnp.stack(bufs_p, axis=0), *outs_s, jnp.stack(bufs_s, axis=0))
```

```python
import functools
import math

import jax
import jax.numpy as jnp
from jax import lax
from jax.experimental import pallas as pl
from jax.experimental.pallas import tpu as pltpu

F32 = jnp.float32
BF16 = jnp.bfloat16

EPS = 1e-6
N_HEADS = 4
DH = 128
GROUP_W = N_HEADS * DH
DH_A = DH // 2
MOBA_BLOCK = 256
MOBA_TOPK = 3
CONV_W = 31
CONV_HALO = 32
N_NORM_GROUPS = 4
NEG_INF = float("-inf")
M_INIT = -1e30
MASK_BIAS = -2.0 ** 100
LOG2E = math.log2(math.e)
SB_DEAD = -150.0
V7X_LANES = 128
V7X_SUBLANES = 8
VMEM_LIMIT_BYTES = 56 * 1024 * 1024

COL_QA, COL_KA, COL_VA, COL_QB, COL_KB, COL_VB, COL_GLU_A, COL_GLU_G, COL_QD, COL_KD, COL_VD = range(11)

_NT = (((1,), (1,)), ((), ()))
_SMALL_ROW_TILES = (512, 256, 128, 64, 32, 16, 8)


def _dot(a, b):
    return jnp.dot(a, b, preferred_element_type=F32)


def _dot_nt(a, b):
    return lax.dot_general(a, b, _NT, preferred_element_type=F32)


def _rms(x, g):
    return x * lax.rsqrt(jnp.mean(x * x, axis=-1, keepdims=True) + EPS) * g


def _shift_div(x, n):
    assert n & (n - 1) == 0
    return lax.shift_right_logical(x, n.bit_length() - 1)


def _pick(n, candidates):
    for c in candidates:
        if n % c == 0:
            return c
    raise ValueError(f"no tile for {n} in {candidates}")


def _params(*sem):
    return pltpu.CompilerParams(dimension_semantics=sem, vmem_limit_bytes=VMEM_LIMIT_BYTES)


def _norm_matmul_kernel(x_ref, g_ref, w_ref, o_ref, *rest):
    h_ref = rest[-1]

    @pl.when(pl.program_id(1) == 0)
    def _():
        h_ref[...] = _rms(x_ref[...], g_ref[...]).astype(BF16)

    y = _dot(h_ref[...], w_ref[...])
    o_ref[...] = y
    if len(rest) == 2:
        rest[0][...] = y.astype(BF16)


def _norm_matmul(x, g, w, with_bf16):
    t, d = x.shape
    n = w.shape[1]
    tm = _pick(t, (1024,) + _SMALL_ROW_TILES)
    tn = _pick(n, (1408, 512, 256, 128))
    out_spec = pl.BlockSpec((tm, tn), lambda i, j: (i, j))
    out_shape = [jax.ShapeDtypeStruct((t, n), F32)]
    if with_bf16:
        out_shape.append(jax.ShapeDtypeStruct((t, n), BF16))
    return pl.pallas_call(
        _norm_matmul_kernel,
        out_shape=out_shape,
        grid=(t // tm, n // tn),
        in_specs=[pl.BlockSpec((tm, d), lambda i, j: (i, 0)),
                  pl.BlockSpec((1, d), lambda i, j: (0, 0)),
                  pl.BlockSpec((d, tn), lambda i, j: (0, j))],
        out_specs=[out_spec] * len(out_shape),
        scratch_shapes=[pltpu.VMEM((tm, d), BF16)],
        compiler_params=_params("parallel", "arbitrary"),
        name="norm_in_proj",
    )(x, g, w)


def _out_proj_kernel(x_ref, a_ref, b_ref, c_ref, d_ref, w_ref, o_ref):
    acc = x_ref[...]
    for gi, m_ref in enumerate((a_ref, b_ref, c_ref, d_ref)):
        acc = acc + _dot(m_ref[...].astype(BF16), w_ref[gi * GROUP_W:(gi + 1) * GROUP_W, :])
    o_ref[...] = acc


def _out_proj(x, mixes, w):
    t, d = x.shape
    tm = _pick(t, (1024,) + _SMALL_ROW_TILES)
    mix_spec = pl.BlockSpec((tm, GROUP_W), lambda i: (i, 0))
    return pl.pallas_call(
        _out_proj_kernel,
        out_shape=jax.ShapeDtypeStruct((t, d), F32),
        grid=(t // tm,),
        in_specs=[pl.BlockSpec((tm, d), lambda i: (i, 0)), mix_spec, mix_spec, mix_spec, mix_spec,
                  pl.BlockSpec(w.shape, lambda i: (0, 0))],
        out_specs=pl.BlockSpec((tm, d), lambda i: (i, 0)),
        compiler_params=_params("parallel"),
        name="out_proj",
    )(x, *mixes, w)


def _ffn_kernel(x_ref, g_ref, wg_ref, wu_ref, wd_ref, gf_ref, o_ref, h_ref, acc_ref, *, final_norm):
    j = pl.program_id(1)

    @pl.when(j == 0)
    def _():
        h_ref[...] = _rms(x_ref[...], g_ref[...]).astype(BF16)
        acc_ref[...] = jnp.zeros_like(acc_ref)

    h = h_ref[...]
    gate = _dot(h, wg_ref[...])
    up = _dot(h, wu_ref[...])
    act = (gate * jax.nn.sigmoid(gate) * up).astype(BF16)
    acc_ref[...] += _dot(act, wd_ref[...])

    @pl.when(j == pl.num_programs(1) - 1)
    def _():
        y = x_ref[...] + acc_ref[...]
        if final_norm:
            y = _rms(y, gf_ref[...])
        o_ref[...] = y


def _ffn(x, g, wg, wu, wd, g_final, final_norm):
    t, d = x.shape
    f = wg.shape[1]
    tm = _pick(t, _SMALL_ROW_TILES)
    tf = _pick(f, (512, 256, 128))
    return pl.pallas_call(
        functools.partial(_ffn_kernel, final_norm=final_norm),
        out_shape=jax.ShapeDtypeStruct((t, d), F32),
        grid=(t // tm, f // tf),
        in_specs=[pl.BlockSpec((tm, d), lambda i, j: (i, 0)),
                  pl.BlockSpec((1, d), lambda i, j: (0, 0)),
                  pl.BlockSpec((d, tf), lambda i, j: (0, j)),
                  pl.BlockSpec((d, tf), lambda i, j: (0, j)),
                  pl.BlockSpec((tf, d), lambda i, j: (j, 0)),
                  pl.BlockSpec((1, d), lambda i, j: (0, 0))],
        out_specs=pl.BlockSpec((tm, d), lambda i, j: (i, 0)),
        scratch_shapes=[pltpu.VMEM((tm, d), BF16), pltpu.VMEM((tm, d), F32)],
        compiler_params=_params("parallel", "arbitrary"),
        name="ffn",
    )(x, g, wg, wu, wd, g_final)


def _lambda(lam_ref, lam_init):
    lp = lam_ref[...]
    s01 = jnp.sum(lp[0:1, :] * lp[1:2, :], axis=1, keepdims=True)
    s23 = jnp.sum(lp[2:3, :] * lp[3:4, :], axis=1, keepdims=True)
    return jnp.exp(s01) - jnp.exp(s23) + lam_init


def _split_maps(q):
    lane = lax.broadcasted_iota(jnp.int32, q.shape, 1)
    return jnp.concatenate([jnp.where(lane < DH_A, q, 0.0), jnp.where(lane >= DH_A, q, 0.0)], axis=0)


def _softmax_update(s, m_ref, l_ref):
    m_prev = m_ref[...]
    m_new = jnp.maximum(m_prev, jnp.max(s, axis=1, keepdims=True))
    alpha = jnp.exp2(m_prev - m_new)
    p = jnp.exp2(s - m_new)
    l_ref[...] = alpha * l_ref[...] + jnp.sum(p, axis=1, keepdims=True)
    m_ref[...] = m_new
    return p, alpha


def _flash_loop(segments, n_total, scores_fn, values_fn, m_ref, acc_ref, s_ref, p_ref, a_ref):
    reps = s_ref.shape[1] // V7X_LANES
    m_ref[...] = jnp.full_like(m_ref, M_INIT)
    acc_ref[...] = jnp.zeros_like(acc_ref)
    p_ref[...] = jnp.zeros_like(p_ref)
    a_ref[...] = jnp.ones_like(a_ref)
    s_ref[...] = scores_fn(0)

    def finish(j):
        acc_ref[:, :DH] = a_ref[...] * acc_ref[:, :DH] + _dot(p_ref[...], values_fn(j))

    def make_body(keep_fn):
        def body(j, carry):
            finish(jnp.maximum(j - 1, 0))
            s = s_ref[...]
            s_ref[...] = scores_fn(jnp.minimum(j + 1, n_total - 1))
            if keep_fn is not None:
                s = jnp.where(keep_fn(j), s, NEG_INF)
            m_prev = m_ref[...]
            m_new = jnp.maximum(m_prev, jnp.max(s, axis=1, keepdims=True))
            alpha = jnp.exp2(m_prev - m_new)
            p = jnp.exp2(s - jnp.concatenate([m_new] * reps, axis=1))
            acc_ref[:, DH:] = alpha * acc_ref[:, DH:] + jnp.sum(p, axis=1, keepdims=True)
            a_ref[...] = alpha
            m_ref[...] = m_new
            p_ref[...] = p.astype(BF16)
            return carry
        return body

    for lo, hi, keep_fn in segments:
        lax.fori_loop(lo, hi, make_body(keep_fn), 0)
    finish(n_total - 1)
    acc = acc_ref[...]
    return acc[:, :DH], acc[:, DH:]


def _flash_scratch(rows, tk):
    return [pltpu.VMEM((rows, V7X_LANES), F32), pltpu.VMEM((rows, 2 * DH), F32),
            pltpu.VMEM((rows, tk), F32), pltpu.VMEM((rows, tk), BF16),
            pltpu.VMEM((rows, V7X_LANES), F32)]


def _softplus(z):
    return jnp.maximum(z, 0.0) + jnp.log(1.0 + jnp.exp(-jnp.abs(z)))


def _later_sum(lk, upper):
    hi = lk.astype(BF16)
    lo = (lk - hi.astype(F32)).astype(BF16)
    return _dot(hi, upper) + _dot(lo, upper)


def _suffix_matrix(n):
    j = lax.broadcasted_iota(jnp.int32, (n, n), 0)
    s = lax.broadcasted_iota(jnp.int32, (n, n), 1)
    return (j > s).astype(BF16)


def _attn_a_kernel(lam_ref, gsub_ref, q_ref, k_ref, v_ref, o_ref, *scratch, tq, tk, lam_init):
    qi = pl.program_id(1)
    qq = _split_maps(q_ref[...] * (DH_A ** -0.5 * LOG2E)).astype(BF16)

    def block(ref, j):
        return ref[pl.ds(pl.multiple_of(j * tk, tk), tk), :]

    def causal(j):
        row = lax.broadcasted_iota(jnp.int32, (2 * tq, tk), 0)
        col = lax.broadcasted_iota(jnp.int32, (2 * tq, tk), 1)
        q_pos = qi * tq + jnp.where(row >= tq, row - tq, row)
        return j * tk + col <= q_pos

    n_full = (qi * tq) // tk
    n_total = (qi * tq + tq - 1) // tk + 1
    acc, l = _flash_loop([(0, n_full, None), (n_full, n_total, causal)], n_total,
                         lambda j: _dot_nt(qq, block(k_ref, j)), lambda j: block(v_ref, j), *scratch)
    lam = _lambda(lam_ref, lam_init)
    o = acc[:tq] / l[:tq] - lam * (acc[tq:] / l[tq:])
    o_ref[...] = (_rms(o, gsub_ref[...]) * (1.0 - lam_init)).astype(o_ref.dtype)


def _attn_a(proj, proj_b, s, lam_a, g_sub, lam_init):
    tq = _pick(s, (256, 128))
    tk = _pick(s, (512, 256, 128))
    return pl.pallas_call(
        functools.partial(_attn_a_kernel, tq=tq, tk=tk, lam_init=lam_init),
        out_shape=jax.ShapeDtypeStruct((s, GROUP_W), BF16),
        grid=(N_HEADS, s // tq),
        in_specs=[pl.BlockSpec(lam_a.shape, lambda h, i: (0, 0)),
                  pl.BlockSpec((1, DH), lambda h, i: (0, 0)),
                  pl.BlockSpec((tq, DH), lambda h, i: (i, COL_QA * N_HEADS + h)),
                  pl.BlockSpec((s, DH), lambda h, i: (0, COL_KA * N_HEADS + h)),
                  pl.BlockSpec((s, DH), lambda h, i: (0, COL_VA * N_HEADS + h))],
        out_specs=pl.BlockSpec((tq, DH), lambda h, i: (i, h)),
        scratch_shapes=_flash_scratch(2 * tq, tk),
        compiler_params=_params("parallel", "arbitrary"),
        name="prompt_diff_attn",
    )(lam_a, g_sub, proj, proj_b, proj_b)


def _block_mean_kernel(k_ref, o_ref, *, n_blocks):
    o_ref[...] = jnp.zeros_like(o_ref)

    def body(b, carry):
        rows = k_ref[pl.ds(pl.multiple_of(b * MOBA_BLOCK, MOBA_BLOCK), MOBA_BLOCK), :]
        o_ref[pl.ds(b, 1), :] = jnp.mean(rows, axis=0, keepdims=True)
        return carry

    lax.fori_loop(0, n_blocks, body, 0)


def _block_means(proj, s, n_pad):
    return pl.pallas_call(
        functools.partial(_block_mean_kernel, n_blocks=s // MOBA_BLOCK),
        out_shape=jax.ShapeDtypeStruct((N_HEADS, n_pad, DH), F32),
        grid=(N_HEADS,),
        in_specs=[pl.BlockSpec((s, DH), lambda h: (0, COL_KB * N_HEADS + h))],
        out_specs=pl.BlockSpec((None, n_pad, DH), lambda h: (h, 0, 0)),
        compiler_params=_params("parallel"),
        name="moba_block_means",
    )(proj)


def _moba_select(gates, own):
    lane = lax.broadcasted_iota(jnp.int32, gates.shape, 1)
    g = jnp.where(lane < own, gates, NEG_INF)
    sel = jnp.where(lane == own, 1.0, 0.0)
    for _ in range(MOBA_TOPK):
        mx = jnp.max(g, axis=1, keepdims=True)
        pick = jnp.min(jnp.where(g == mx, lane, gates.shape[1]), axis=1, keepdims=True)
        hit = jnp.logical_and(lane == pick, mx > NEG_INF)
        sel = jnp.where(hit, 1.0, sel)
        g = jnp.where(lane == pick, NEG_INF, g)
    return sel


def _attn_b_kernel(q_ref, kmean_ref, k_ref, v_ref, o_ref, *scratch, t):
    qi = pl.program_id(1)
    per_tile = t // MOBA_BLOCK
    q_raw = q_ref[...]
    gates = lax.dot_general(q_raw, kmean_ref[...], _NT, precision=lax.Precision.HIGHEST,
                            preferred_element_type=F32)
    row1 = lax.broadcasted_iota(jnp.int32, (t, 1), 0)
    own = qi * per_tile + _shift_div(row1, MOBA_BLOCK)
    sel = _moba_select(gates, own)
    q = jnp.concatenate([(q_raw * (DH ** -0.5 * LOG2E)).astype(BF16),
                         jnp.where(sel > 0.5, 0.0, MASK_BIAS).astype(BF16)], axis=1)

    def block(ref, j):
        return ref[pl.ds(pl.multiple_of(j * t, t), t), :]

    def scores(j):
        key = lax.broadcasted_iota(jnp.int32, (t, sel.shape[1]), 0)
        blk = lax.broadcasted_iota(jnp.int32, (t, sel.shape[1]), 1)
        onehot = (blk == j * per_tile + _shift_div(key, MOBA_BLOCK)).astype(BF16)
        return _dot_nt(q, jnp.concatenate([block(k_ref, j), onehot], axis=1))

    def causal(j):
        row = lax.broadcasted_iota(jnp.int32, (t, t), 0)
        col = lax.broadcasted_iota(jnp.int32, (t, t), 1)
        return col <= row

    acc, l = _flash_loop([(0, qi, None), (qi, qi + 1, causal)], qi + 1,
                         scores, lambda j: block(v_ref, j), *scratch)
    o_ref[...] = (acc / l).astype(o_ref.dtype)


def _attn_b(proj, proj_b, s):
    t = _pick(s, (512, 256))
    n_pad = -(-(s // MOBA_BLOCK) // V7X_LANES) * V7X_LANES
    kmean = _block_means(proj, s, n_pad)
    return pl.pallas_call(
        functools.partial(_attn_b_kernel, t=t),
        out_shape=jax.ShapeDtypeStruct((s, GROUP_W), BF16),
        grid=(N_HEADS, s // t),
        in_specs=[pl.BlockSpec((t, DH), lambda h, i: (i, COL_QB * N_HEADS + h)),
                  pl.BlockSpec((None, n_pad, DH), lambda h, i: (h, 0, 0)),
                  pl.BlockSpec((s, DH), lambda h, i: (0, COL_KB * N_HEADS + h)),
                  pl.BlockSpec((s, DH), lambda h, i: (0, COL_VB * N_HEADS + h))],
        out_specs=pl.BlockSpec((t, DH), lambda h, i: (i, h)),
        scratch_shapes=_flash_scratch(t, t),
        compiler_params=_params("parallel", "arbitrary"),
        name="prompt_moba",
    )(proj, kmean, proj_b, proj_b)


def _attn_d_kernel(q_ref, k_ref, v_ref, o_ref, r_ref, acc_ref, *, tq, tk):
    qi = pl.program_id(1)
    q = (q_ref[...] * (DH ** -0.5)).astype(BF16)
    r_ref[...] = jnp.zeros_like(r_ref)
    acc_ref[...] = jnp.zeros_like(acc_ref)

    def cond(carry):
        kb, r_max = carry
        return jnp.logical_and(kb >= 0, r_max > SB_DEAD)

    def body(carry):
        kb, _ = carry
        start = pl.multiple_of(kb * tk, tk)
        z = _dot_nt(q, k_ref[pl.ds(start, tk), :])
        row = lax.broadcasted_iota(jnp.int32, (tq, tk), 0)
        col = lax.broadcasted_iota(jnp.int32, (tq, tk), 1)
        strict = start + col < qi * tq + row
        sp = _softplus(z)
        lk = jnp.where(strict, -sp, 0.0)
        later = _later_sum(lk, _suffix_matrix(tk))
        r = r_ref[...]
        w = jnp.where(strict, jnp.exp(z - sp + later + r), 0.0)
        acc_ref[...] += _dot(w.astype(BF16), v_ref[pl.ds(start, tk), :])
        r_new = r + jnp.sum(lk, axis=1, keepdims=True)
        r_ref[...] = r_new
        return kb - 1, jnp.max(r_new)

    lax.while_loop(cond, body, ((qi * tq + tq - 1) // tk, jnp.float32(0.0)))
    o_ref[...] = acc_ref[...].astype(o_ref.dtype)


def _attn_d(proj, proj_b, s):
    tq = _pick(s, (256, 128))
    tk = _pick(s, (512, 256, 128))
    return pl.pallas_call(
        functools.partial(_attn_d_kernel, tq=tq, tk=tk),
        out_shape=jax.ShapeDtypeStruct((s, GROUP_W), BF16),
        grid=(N_HEADS, s // tq),
        in_specs=[pl.BlockSpec((tq, DH), lambda h, i: (i, COL_QD * N_HEADS + h)),
                  pl.BlockSpec((s, DH), lambda h, i: (0, COL_KD * N_HEADS + h)),
                  pl.BlockSpec((s, DH), lambda h, i: (0, COL_VD * N_HEADS + h))],
        out_specs=pl.BlockSpec((tq, DH), lambda h, i: (i, h)),
        scratch_shapes=[pltpu.VMEM((tq, 1), F32), pltpu.VMEM((tq, DH), F32)],
        compiler_params=_params("parallel", "arbitrary"),
        name="prompt_stickbreak",
    )(proj, proj_b, proj_b)


def _conv_kernel(*refs, tt, has_prev):
    if has_prev:
        cur_ref, prev_ref, cbuf_ref, bglu_ref, wdw_ref, bdw_ref, gn_ref, bn_ref, o_ref, nbuf_ref, ubuf_ref = refs
    else:
        cur_ref, cbuf_ref, bglu_ref, wdw_ref, bdw_ref, gn_ref, bn_ref, o_ref, nbuf_ref, ubuf_ref = refs
    i = pl.program_id(1)
    c = GROUP_W

    def glu(x):
        x = x + bglu_ref[...]
        return x[:, :c] * jax.nn.sigmoid(x[:, c:])

    ubuf_ref[CONV_HALO:CONV_HALO + tt, :] = glu(cur_ref[...])

    @pl.when(i == 0)
    def _():
        ubuf_ref[CONV_HALO - (CONV_W - 1):CONV_HALO, :] = cbuf_ref[...]

    if has_prev:
        @pl.when(i > 0)
        def _():
            ubuf_ref[0:CONV_HALO, :] = glu(prev_ref[...])

    off = CONV_HALO - (CONV_W - 1)
    y = jnp.zeros((tt, c), F32) + bdw_ref[...]
    for j in range(CONV_W):
        y = y + ubuf_ref[off + j:off + j + tt, :] * wdw_ref[j:j + 1, :]

    gw = c // N_NORM_GROUPS
    for gi in range(N_NORM_GROUPS):
        seg = y[:, gi * gw:(gi + 1) * gw]
        mu = jnp.mean(seg, axis=-1, keepdims=True)
        var = jnp.mean(jnp.square(seg - mu), axis=-1, keepdims=True)
        yn = (seg - mu) * lax.rsqrt(var + EPS)
        yn = yn * gn_ref[:, gi * gw:(gi + 1) * gw] + bn_ref[:, gi * gw:(gi + 1) * gw]
        o_ref[:, gi * gw:(gi + 1) * gw] = (yn * jax.nn.sigmoid(yn)).astype(o_ref.dtype)

    @pl.when(i == pl.num_programs(1) - 1)
    def _():
        nbuf_ref[...] = ubuf_ref[CONV_HALO + tt - (CONV_W - 1):CONV_HALO + tt, :]


def _conv_group(proj, batch, s, conv_buf, b_glu, w_dw, b_dw, g_norm, b_norm, out_dtype):
    tt = _pick(s, _SMALL_ROW_TILES)
    nt = s // tt
    has_prev = nt > 1
    assert tt + CONV_HALO >= CONV_W - 1 and (not has_prev or tt % CONV_HALO == 0)
    c = GROUP_W
    glu_col = COL_GLU_A // 2
    in_specs = [pl.BlockSpec((tt, 2 * c), lambda b, i: (b * nt + i, glu_col))]
    args = [proj]
    if has_prev:
        per = tt // CONV_HALO
        in_specs.append(pl.BlockSpec((CONV_HALO, 2 * c),
                                     lambda b, i: (jnp.maximum((b * nt + i) * per - 1, 0), glu_col)))
        args.append(proj)
    vec = lambda n: pl.BlockSpec((1, n), lambda b, i: (0, 0))
    in_specs += [pl.BlockSpec((None, CONV_W - 1, c), lambda b, i: (b, 0, 0)),
                 vec(2 * c), pl.BlockSpec((CONV_W, c), lambda b, i: (0, 0)), vec(c), vec(c), vec(c)]
    args += [conv_buf, b_glu, w_dw, b_dw, g_norm, b_norm]
    return pl.pallas_call(
        functools.partial(_conv_kernel, tt=tt, has_prev=has_prev),
        out_shape=(jax.ShapeDtypeStruct((batch * s, c), out_dtype),
                   jax.ShapeDtypeStruct((batch, CONV_W - 1, c), F32)),
        grid=(batch, nt),
        in_specs=in_specs,
        out_specs=(pl.BlockSpec((tt, c), lambda b, i: (b * nt + i, 0)),
                   pl.BlockSpec((None, CONV_W - 1, c), lambda b, i: (b, 0, 0))),
        scratch_shapes=[pltpu.VMEM((CONV_HALO + tt, c), F32)],
        compiler_params=_params("parallel", "arbitrary"),
        name="conv_module",
    )(*args)


def _page_heads(ref, page, dtype):
    return [ref[pl.ds(h, page, stride=N_HEADS), :].astype(dtype) for h in range(N_HEADS)]


def _pad_heads(pad_ref, new_ref, dtype):
    pad_ref[...] = jnp.zeros_like(pad_ref)
    pad_ref[0:new_ref.shape[0], :] = new_ref[...]
    return [pad_ref[:, h * DH:(h + 1) * DH].astype(dtype) for h in range(N_HEADS)]


def _head_rows(fn):
    return jnp.concatenate([fn(h) for h in range(N_HEADS)], axis=0)


def _new_key_index(shape, nq):
    row = lax.broadcasted_iota(jnp.int32, shape, 0)
    col = lax.broadcasted_iota(jnp.int32, shape, 1)
    return col, row & (nq - 1)


def _dec_a_kernel(pt_ref, lam_ref, gsub_ref, q_ref, kn_ref, vn_ref, *rest, n_pg, nq, page, lam_init):
    del pt_ref
    k_pages, v_pages = rest[:n_pg], rest[n_pg:2 * n_pg]
    o_ref, qh_ref, kpad_ref, vpad_ref, m_ref, l_ref, acc_ref = rest[2 * n_pg:]
    g = pl.program_id(1)
    rh = 2 * nq

    def reduce(s, pv_fn):
        p, alpha = _softmax_update(s, m_ref, l_ref)
        pv = _head_rows(lambda h: pv_fn(h, p[h * rh:(h + 1) * rh].astype(BF16)))
        acc_ref[...] = alpha * acc_ref[...] + pv

    @pl.when(g == 0)
    def _():
        for h in range(N_HEADS):
            qh = q_ref[:, h * DH:(h + 1) * DH] * (DH_A ** -0.5 * LOG2E)
            qh_ref[h] = _split_maps(qh).astype(BF16)
        m_ref[...] = jnp.full_like(m_ref, M_INIT)
        l_ref[...] = jnp.zeros_like(l_ref)
        acc_ref[...] = jnp.zeros_like(acc_ref)
        kh = _pad_heads(kpad_ref, kn_ref, BF16)
        vh = _pad_heads(vpad_ref, vn_ref, BF16)
        s = _head_rows(lambda h: _dot_nt(qh_ref[h], kh[h]))
        key, qry = _new_key_index(s.shape, nq)
        reduce(jnp.where(key <= qry, s, NEG_INF), lambda h, ph: _dot(ph, vh[h]))

    kh = [_page_heads(r, page, BF16) for r in k_pages]
    vh = [_page_heads(r, page, BF16) for r in v_pages]
    keys_of = lambda pages, h: jnp.concatenate([pages[t][h] for t in range(n_pg)], axis=0)
    s = _head_rows(lambda h: _dot_nt(qh_ref[h], keys_of(kh, h)))
    reduce(s, lambda h, ph: _dot(ph, keys_of(vh, h)))

    @pl.when(g == pl.num_programs(1) - 1)
    def _():
        acc = acc_ref[...]
        l = l_ref[...]
        lam = _lambda(lam_ref, lam_init)
        for h in range(N_HEADS):
            r1, r2 = h * rh, h * rh + nq
            o = acc[r1:r1 + nq] / l[r1:r1 + nq] - lam * (acc[r2:r2 + nq] / l[r2:r2 + nq])
            o_ref[:, h * DH:(h + 1) * DH] = _rms(o, gsub_ref[...]) * (1.0 - lam_init)


def _stickbreak_pages(qh_ref, k_pages, v_pages, r_ref, acc_ref, rmax_ref, nq, page):
    def step(kh, vh, strict, r):
        z = _head_rows(lambda h: _dot_nt(qh_ref[h], kh[h]))
        sp = _softplus(z)
        lk = -sp if strict is None else jnp.where(strict(z.shape), -sp, 0.0)
        later = _later_sum(lk, _suffix_matrix(page))
        w = jnp.exp(z - sp + later + r)
        if strict is not None:
            w = jnp.where(strict(z.shape), w, 0.0)
        acc_ref[...] += _head_rows(lambda h: _dot(w[h * nq:(h + 1) * nq].astype(BF16), vh[h]))
        r_new = r + jnp.sum(lk, axis=1, keepdims=True)
        r_ref[...] = r_new
        rmax_ref[0] = jnp.max(r_new)

    def run_pages():
        for k_ref, v_ref in zip(k_pages, v_pages):
            @pl.when(rmax_ref[0] > SB_DEAD)
            def _():
                step(_page_heads(k_ref, page, BF16), _page_heads(v_ref, page, BF16), None, r_ref[...])

    return step, run_pages


def _dec_d_head_kernel(pt_ref, q_ref, kn_ref, vn_ref, *rest, n_pg, nq, page):
    del pt_ref
    k_pages, v_pages = rest[:n_pg], rest[n_pg:2 * n_pg]
    acc_out, r_out, qh_ref, kpad_ref, vpad_ref, r_ref, acc_ref, rmax_ref = rest[2 * n_pg:]
    step, run_pages = _stickbreak_pages(qh_ref, k_pages, v_pages, r_ref, acc_ref, rmax_ref, nq, page)
    for h in range(N_HEADS):
        qh_ref[h] = (q_ref[:, h * DH:(h + 1) * DH] * (DH ** -0.5)).astype(BF16)
    acc_ref[...] = jnp.zeros_like(acc_ref)

    def strict(shape):
        key, qry = _new_key_index(shape, nq)
        return key < qry

    step(_pad_heads(kpad_ref, kn_ref, BF16), _pad_heads(vpad_ref, vn_ref, BF16), strict,
         jnp.zeros(r_ref.shape, F32))
    run_pages()
    acc_out[...] = acc_ref[...]
    r_out[...] = jnp.broadcast_to(r_ref[...], r_out.shape)


def _dec_d_tail_kernel(pt_ref, dead_ref, q_ref, acc_in, r_in, *rest, n_pg, nq, page):
    del pt_ref
    k_pages, v_pages = rest[:n_pg], rest[n_pg:2 * n_pg]
    o_ref, qh_ref, r_ref, acc_ref, rmax_ref = rest[2 * n_pg:]
    g = pl.program_id(1)
    _, run_pages = _stickbreak_pages(qh_ref, k_pages, v_pages, r_ref, acc_ref, rmax_ref, nq, page)

    @pl.when(g == 0)
    def _():
        for h in range(N_HEADS):
            qh_ref[h] = (q_ref[:, h * DH:(h + 1) * DH] * (DH ** -0.5)).astype(BF16)
        acc_ref[...] = acc_in[...]
        r = r_in[:, 0:1]
        r_ref[...] = r
        rmax_ref[0] = jnp.where(dead_ref[pl.program_id(0)] == 1, NEG_INF, jnp.max(r))

    run_pages()

    @pl.when(g == pl.num_programs(1) - 1)
    def _():
        acc = acc_ref[...]
        for h in range(N_HEADS):
            o_ref[:, h * DH:(h + 1) * DH] = acc[h * nq:(h + 1) * nq]


def _dec_b_kernel(pt_ref, q_ref, kn_ref, vn_ref, *rest, n_pg, nq, page, n_blocks):
    del pt_ref
    k_pages, v_pages = rest[:n_pg], rest[n_pg:2 * n_pg]
    o_ref, qf_ref, qh_ref, kpad_ref, vpad_ref, g_ref, m_ref, l_ref, acc_ref = rest[2 * n_pg:]
    g = pl.program_id(1)
    per_block = MOBA_BLOCK // page
    lane = lax.broadcasted_iota(jnp.int32, g_ref.shape, 1)

    @pl.when(g == 0)
    def _():
        for h in range(N_HEADS):
            qh = q_ref[:, h * DH:(h + 1) * DH]
            qf_ref[h] = qh
            qh_ref[h] = (qh * (DH ** -0.5 * LOG2E)).astype(BF16)
        g_ref[...] = jnp.full_like(g_ref, NEG_INF)
        m_ref[...] = jnp.full_like(m_ref, M_INIT)
        l_ref[...] = jnp.zeros_like(l_ref)

    gates, maxes, sums = g_ref[...], m_ref[...], l_ref[...]
    kf = [_page_heads(r, page, F32) for r in k_pages]
    vh = [_page_heads(r, page, BF16) for r in v_pages]
    s_all = _head_rows(lambda h: _dot_nt(
        qh_ref[h], jnp.concatenate([kf[t][h].astype(BF16) for t in range(n_pg)], axis=0)))
    for blk in range(n_pg // per_block):
        pages = range(blk * per_block, (blk + 1) * per_block)

        def gate_of(h):
            k_mean = sum(jnp.sum(kf[t][h], axis=0, keepdims=True) for t in pages)
            return jnp.sum(qf_ref[h] * (k_mean * (1.0 / MOBA_BLOCK)), axis=1, keepdims=True)

        gate = _head_rows(gate_of)
        s = s_all[:, blk * MOBA_BLOCK:(blk + 1) * MOBA_BLOCK]
        m_b = jnp.max(s, axis=1, keepdims=True)
        p = jnp.exp2(s - m_b)
        l_b = jnp.sum(p, axis=1, keepdims=True)
        acc_b = _head_rows(lambda h: _dot(p[h * nq:(h + 1) * nq].astype(BF16),
                                          jnp.concatenate([vh[t][h] for t in pages], axis=0)))
        b_idx = g * (n_pg // per_block) + blk
        acc_ref[b_idx] = acc_b
        here = lane == b_idx
        gates = jnp.where(here, gate, gates)
        maxes = jnp.where(here, m_b, maxes)
        sums = jnp.where(here, l_b, sums)
    g_ref[...], m_ref[...], l_ref[...] = gates, maxes, sums

    @pl.when(g == pl.num_programs(1) - 1)
    def _():
        kh = _pad_heads(kpad_ref, kn_ref, BF16)
        vh = _pad_heads(vpad_ref, vn_ref, BF16)
        s = _head_rows(lambda h: _dot_nt(qh_ref[h], kh[h]))
        key, qry = _new_key_index(s.shape, nq)
        s = jnp.where(key <= qry, s, NEG_INF)
        m_o = jnp.max(s, axis=1, keepdims=True)
        p = jnp.exp2(s - m_o)
        chosen = jnp.logical_and(_moba_select(gates, n_blocks) > 0.5, lane < n_blocks)
        m_all = jnp.maximum(m_o, jnp.max(jnp.where(chosen, maxes, NEG_INF), axis=1, keepdims=True))
        weight = jnp.where(chosen, jnp.exp2(maxes - m_all), 0.0)
        w_o = jnp.exp2(m_o - m_all)
        den = w_o * jnp.sum(p, axis=1, keepdims=True) + jnp.sum(weight * sums, axis=1, keepdims=True)

        num = w_o * _head_rows(lambda h: _dot(p[h * nq:(h + 1) * nq].astype(BF16), vh[h]))
        for b in range(n_blocks):
            w_b = jnp.sum(jnp.where(lane == b, weight, 0.0), axis=1, keepdims=True)
            num = num + w_b * acc_ref[b]
        out = num / den
        for h in range(N_HEADS):
            o_ref[:, h * DH:(h + 1) * DH] = out[h * nq:(h + 1) * nq]


def _dec_attention(kind, proj, k_cache, v_cache, page_table, li, nq, cols, extra=(), **kw):
    n_seq, n_pages = page_table.shape
    page = k_cache.shape[2] // N_HEADS
    n_pg = _pick(n_pages, (16, 8, 4, 2))
    steps = n_pages // n_pg
    col_q, col_k, col_v = cols
    w = GROUP_W
    assert nq % V7X_SUBLANES == 0 and nq & (nq - 1) == 0 and nq <= page
    assert (n_pages * page) % MOBA_BLOCK == 0 and MOBA_BLOCK % page == 0

    def page_spec(t):
        return pl.BlockSpec((None, None, page * N_HEADS, DH),
                            lambda b, g, pt: (li, pt[b, g * n_pg + t], 0, 0))

    small = lambda a: pl.BlockSpec(a.shape, lambda b, g, pt: (0,) * a.ndim)
    row_spec = lambda col: pl.BlockSpec((nq, w), lambda b, g, pt: (b, col))
    in_specs = ([small(a) for a in extra] + [row_spec(col_q), row_spec(col_k), row_spec(col_v)]
                + [page_spec(t) for t in range(n_pg)] * 2)
    args = list(extra) + [proj, proj, proj] + [k_cache] * n_pg + [v_cache] * n_pg

    pad = [pltpu.VMEM((page, w), F32), pltpu.VMEM((page, w), F32)]
    stat = lambda rows: pltpu.VMEM((rows, 1), F32)
    if kind == "a":
        rows = 2 * N_HEADS * nq
        body = functools.partial(_dec_a_kernel, n_pg=n_pg, nq=nq, page=page, **kw)
        scratch = [pltpu.VMEM((N_HEADS, 2 * nq, DH), BF16)] + pad + [
            stat(rows), stat(rows), pltpu.VMEM((rows, DH), F32)]
    else:
        rows = N_HEADS * nq
        n_blocks = n_pages * page // MOBA_BLOCK
        assert n_blocks <= V7X_LANES
        body = functools.partial(_dec_b_kernel, n_pg=n_pg, nq=nq, page=page, n_blocks=n_blocks)
        lanes = lambda: pltpu.VMEM((rows, V7X_LANES), F32)
        scratch = [pltpu.VMEM((N_HEADS, nq, DH), F32), pltpu.VMEM((N_HEADS, nq, DH), BF16)] + pad + [
            lanes(), lanes(), lanes(), pltpu.VMEM((n_blocks, rows, DH), F32)]

    return pl.pallas_call(
        body,
        out_shape=jax.ShapeDtypeStruct((n_seq * nq, w), F32),
        grid_spec=pltpu.PrefetchScalarGridSpec(
            num_scalar_prefetch=1,
            grid=(n_seq, steps),
            in_specs=in_specs,
            out_specs=pl.BlockSpec((nq, w), lambda b, g, pt: (b, 0)),
            scratch_shapes=scratch),
        compiler_params=_params("parallel", "arbitrary"),
        name="sample_attn_" + kind,
    )(page_table, *args)


def _dec_stickbreak(proj, k_cache, v_cache, page_table, li, nq):
    n_seq, n_pages = page_table.shape
    page = k_cache.shape[2] // N_HEADS
    n_head = 8 if n_pages >= 16 else n_pages // 2
    n_pg = _pick(n_pages - n_head, (8, 4, 2, 1))
    steps = (n_pages - n_head) // n_pg
    rows, w = N_HEADS * nq, GROUP_W
    assert 0 < n_head < n_pages
    page_block = (None, None, page * N_HEADS, DH)
    qh = pltpu.VMEM((N_HEADS, nq, DH), BF16)
    state = [pltpu.VMEM((rows, 1), F32), pltpu.VMEM((rows, DH), F32), pltpu.SMEM((1,), F32)]

    def head_page(t):
        return pl.BlockSpec(page_block, lambda b, pt: (li, pt[b, n_pages - 1 - t], 0, 0))

    row_spec = lambda col: pl.BlockSpec((nq, w), lambda b, pt: (b, col))
    acc0, r0 = pl.pallas_call(
        functools.partial(_dec_d_head_kernel, n_pg=n_head, nq=nq, page=page),
        out_shape=[jax.ShapeDtypeStruct((n_seq * rows, DH), F32),
                   jax.ShapeDtypeStruct((n_seq * rows, V7X_LANES), F32)],
        grid_spec=pltpu.PrefetchScalarGridSpec(
            num_scalar_prefetch=1,
            grid=(n_seq,),
            in_specs=[row_spec(COL_QD), row_spec(COL_KD), row_spec(COL_VD)]
            + [head_page(t) for t in range(n_head)] * 2,
            out_specs=[pl.BlockSpec((rows, DH), lambda b, pt: (b, 0)),
                       pl.BlockSpec((rows, V7X_LANES), lambda b, pt: (b, 0))],
            scratch_shapes=[qh, pltpu.VMEM((page, w), F32), pltpu.VMEM((page, w), F32)] + state),
        compiler_params=_params("parallel"),
        name="sample_stickbreak_head",
    )(page_table, proj, proj, proj, *([k_cache] * n_head), *([v_cache] * n_head))

    dead = (jnp.max(r0.reshape(n_seq, -1), axis=1) < SB_DEAD).astype(jnp.int32)

    def tail_page(t):
        def imap(b, g, pt, dd):
            idx = n_pages - 1 - n_head - (g * n_pg + t)
            return (li, jnp.where(dd[b] == 1, 0, pt[b, idx]), 0, 0)
        return pl.BlockSpec(page_block, imap)

    tail = pl.pallas_call(
        functools.partial(_dec_d_tail_kernel, n_pg=n_pg, nq=nq, page=page),
        out_shape=jax.ShapeDtypeStruct((n_seq * nq, w), F32),
        grid_spec=pltpu.PrefetchScalarGridSpec(
            num_scalar_prefetch=2,
            grid=(n_seq, steps),
            in_specs=[pl.BlockSpec((nq, w), lambda b, g, pt, dd: (b, COL_QD)),
                      pl.BlockSpec((rows, DH), lambda b, g, pt, dd: (b, 0)),
                      pl.BlockSpec((rows, V7X_LANES), lambda b, g, pt, dd: (b, 0))]
            + [tail_page(t) for t in range(n_pg)] * 2,
            out_specs=pl.BlockSpec((nq, w), lambda b, g, pt, dd: (b, 0)),
            scratch_shapes=[qh] + state),
        compiler_params=_params("parallel", "arbitrary"),
        name="sample_stickbreak_tail",
    )

    def run_tail():
        return tail(page_table, dead, proj, acc0, r0, *([k_cache] * n_pg), *([v_cache] * n_pg))

    def all_dead():
        return acc0.reshape(n_seq, N_HEADS, nq, DH).transpose(0, 2, 1, 3).reshape(n_seq * nq, w)

    return lax.cond(jnp.any(dead == 0), run_tail, all_dead)


KV_COLS = (COL_KA, COL_VA, COL_KB, COL_VB, COL_KD, COL_VD)


def _new_rows_kernel(*refs, n_layers, tm):
    n_in = n_layers * len(KV_COLS)
    ins, outs = refs[:n_in], refs[n_in:]
    for li in range(n_layers):
        @pl.when(pl.program_id(0) == li)
        def _():
            for gi, o_ref in enumerate(outs):
                src = ins[li * len(KV_COLS) + gi]
                for h in range(N_HEADS):
                    o_ref[pl.ds(h, tm, stride=N_HEADS), :] = src[:, h * DH:(h + 1) * DH]


def _new_rows(projs):
    n_layers = len(projs)
    t = projs[0].shape[0]
    tm = _pick(t, _SMALL_ROW_TILES)
    nt = t // tm

    def in_spec(li, col):
        def imap(d, i):
            return (jnp.where(d == li, i, jnp.where(d > li, nt - 1, 0)), col)
        return pl.BlockSpec((tm, GROUP_W), imap)

    in_specs = [in_spec(li, col) for li in range(n_layers) for col in KV_COLS]
    args = [projs[li] for li in range(n_layers) for _ in KV_COLS]
    out_shape = [jax.ShapeDtypeStruct((n_layers, t * N_HEADS, DH), F32)] * len(KV_COLS)
    out_specs = [pl.BlockSpec((None, tm * N_HEADS, DH), lambda d, i: (d, i, 0))] * len(KV_COLS)
    return pl.pallas_call(
        functools.partial(_new_rows_kernel, n_layers=n_layers, tm=tm),
        out_shape=out_shape,
        grid=(n_layers, nt),
        in_specs=in_specs,
        out_specs=out_specs,
        compiler_params=_params("arbitrary", "arbitrary"),
        name="new_kv_rows",
    )(*args)


def kernel(x_prompt, x_sample, cache_a_k, cache_a_v, cache_b_k, cache_b_v, cache_d_k, cache_d_v, state_c_conv, page_table, g_mix, w_in, lam_a, g_a_sub, b_c_glu, w_c_dw, b_c_dw, g_c_norm, b_c_norm, w_out, g_ffn, w_gate, w_up, w_down, g_final):
    n_b, s, d = x_prompt.shape
    n_seq, nq, _ = x_sample.shape
    depth = w_in.shape[0]
    assert n_b == 1, "prompt kernels keep one sequence's keys resident"

    def paged(c):
        assert c.shape[3:] == (N_HEADS, DH)
        return c.reshape(c.shape[0], c.shape[1], c.shape[2] * N_HEADS, DH)

    caches = [paged(c) for c in (cache_a_k, cache_a_v, cache_b_k, cache_b_v, cache_d_k, cache_d_v)]
    row = lambda v: v.reshape(1, -1)
    zero_buf = jnp.zeros((n_b, CONV_W - 1, GROUP_W), F32)

    xp = x_prompt.reshape(n_b * s, d)
    xs = x_sample.reshape(n_seq * nq, d)
    projs_p, projs_s, bufs_p, bufs_s = [], [], [], []

    for li in range(depth):
        w_in_l, w_out_l = w_in[li].astype(BF16), w_out[li].astype(BF16)
        wg_l, wu_l, wd_l = w_gate[li].astype(BF16), w_up[li].astype(BF16), w_down[li].astype(BF16)
        lam_init = 0.8 - 0.6 * math.exp(-0.3 * li)
        conv_w = (row(b_c_glu[li]), w_c_dw[li], row(b_c_dw[li]), row(g_c_norm[li]), row(b_c_norm[li]))
        last = li == depth - 1

        proj, proj_b = _norm_matmul(xp, row(g_mix[li]), w_in_l, with_bf16=True)
        o_a = _attn_a(proj, proj_b, s, lam_a[li], row(g_a_sub[li]), lam_init)
        o_b = _attn_b(proj, proj_b, s)
        o_c, buf_p = _conv_group(proj, n_b, s, zero_buf, *conv_w, out_dtype=BF16)
        o_d = _attn_d(proj, proj_b, s)
        xp = _out_proj(xp, (o_a, o_b, o_c, o_d), w_out_l)
        xp = _ffn(xp, row(g_ffn[li]), wg_l, wu_l, wd_l, row(g_final), final_norm=last)
        projs_p.append(proj)
        bufs_p.append(buf_p)

        (proj,) = _norm_matmul(xs, row(g_mix[li]), w_in_l, with_bf16=False)
        o_a = _dec_attention("a", proj, caches[0], caches[1], page_table, li, nq,
                             (COL_QA, COL_KA, COL_VA), extra=(lam_a[li], row(g_a_sub[li])),
                             lam_init=lam_init)
        o_b = _dec_attention("b", proj, caches[2], caches[3], page_table, li, nq,
                             (COL_QB, COL_KB, COL_VB))
        o_c, buf_s = _conv_group(proj, n_seq, nq, state_c_conv[li], *conv_w, out_dtype=F32)
        o_d = _dec_stickbreak(proj, caches[4], caches[5], page_table, li, nq)
        xs = _out_proj(xs, (o_a, o_b, o_c, o_d), w_out_l)
        xs = _ffn(xs, row(g_ffn[li]), wg_l, wu_l, wd_l, row(g_final), final_norm=last)
        projs_s.append(proj)
        bufs_s.append(buf_s)

    outs_p = [o.reshape(depth, n_b, s, N_HEADS, DH) for o in _new_rows(projs_p)]
    outs_s = [o.reshape(depth, n_seq, nq, N_HEADS, DH) for o in _new_rows(projs_s)]
    return (xp.reshape(n_b, s, d), xs.reshape(n_seq, nq, d),
            *outs_p, jnp.stack(bufs_p, axis=0), *outs_s, jnp.stack(bufs_s, axis=0))
```

```python
import functools
import math

import jax
import jax.numpy as jnp
from jax import lax
from jax.experimental import pallas as pl
from jax.experimental.pallas import tpu as pltpu

F32 = jnp.float32
BF16 = jnp.bfloat16

EPS = 1e-6
N_HEADS = 4
DH = 128
GROUP_W = N_HEADS * DH
DH_A = DH // 2
MOBA_BLOCK = 256
MOBA_TOPK = 3
CONV_W = 31
CONV_HALO = 32
N_NORM_GROUPS = 4
NEG_INF = float("-inf")
M_INIT = -1e30
MASK_BIAS = -2.0 ** 100
LOG2E = math.log2(math.e)
SB_DEAD = -150.0
V7X_LANES = 128
V7X_SUBLANES = 8
VMEM_LIMIT_BYTES = 56 * 1024 * 1024

COL_QA, COL_KA, COL_VA, COL_QB, COL_KB, COL_VB, COL_GLU_A, COL_GLU_G, COL_QD, COL_KD, COL_VD = range(11)

_NT = (((1,), (1,)), ((), ()))
_SMALL_ROW_TILES = (512, 256, 128, 64, 32, 16, 8)


def _dot(a, b):
    return jnp.dot(a, b, preferred_element_type=F32)


def _dot_nt(a, b):
    return lax.dot_general(a, b, _NT, preferred_element_type=F32)


def _rms(x, g):
    return x * lax.rsqrt(jnp.mean(x * x, axis=-1, keepdims=True) + EPS) * g


def _shift_div(x, n):
    assert n & (n - 1) == 0
    return lax.shift_right_logical(x, n.bit_length() - 1)


def _pick(n, candidates):
    for c in candidates:
        if n % c == 0:
            return c
    raise ValueError(f"no tile for {n} in {candidates}")


def _params(*sem):
    return pltpu.CompilerParams(dimension_semantics=sem, vmem_limit_bytes=VMEM_LIMIT_BYTES)


def _norm_matmul_kernel(x_ref, g_ref, w_ref, o_ref, *rest):
    h_ref = rest[-1]

    @pl.when(pl.program_id(1) == 0)
    def _():
        h_ref[...] = _rms(x_ref[...], g_ref[...]).astype(BF16)

    y = _dot(h_ref[...], w_ref[...])
    o_ref[...] = y
    if len(rest) == 2:
        rest[0][...] = y.astype(BF16)


def _norm_matmul(x, g, w, with_bf16):
    t, d = x.shape
    n = w.shape[1]
    tm = _pick(t, (1024,) + _SMALL_ROW_TILES)
    tn = _pick(n, (1408, 512, 256, 128))
    out_spec = pl.BlockSpec((tm, tn), lambda i, j: (i, j))
    out_shape = [jax.ShapeDtypeStruct((t, n), F32)]
    if with_bf16:
        out_shape.append(jax.ShapeDtypeStruct((t, n), BF16))
    return pl.pallas_call(
        _norm_matmul_kernel,
        out_shape=out_shape,
        grid=(t // tm, n // tn),
        in_specs=[pl.BlockSpec((tm, d), lambda i, j: (i, 0)),
                  pl.BlockSpec((1, d), lambda i, j: (0, 0)),
                  pl.BlockSpec((d, tn), lambda i, j: (0, j))],
        out_specs=[out_spec] * len(out_shape),
        scratch_shapes=[pltpu.VMEM((tm, d), BF16)],
        compiler_params=_params("parallel", "arbitrary"),
        name="norm_in_proj",
    )(x, g, w)


def _out_proj_kernel(x_ref, a_ref, b_ref, c_ref, d_ref, w_ref, o_ref):
    acc = x_ref[...]
    for gi, m_ref in enumerate((a_ref, b_ref, c_ref, d_ref)):
        acc = acc + _dot(m_ref[...].astype(BF16), w_ref[gi * GROUP_W:(gi + 1) * GROUP_W, :])
    o_ref[...] = acc


def _out_proj(x, mixes, w):
    t, d = x.shape
    tm = _pick(t, (1024,) + _SMALL_ROW_TILES)
    mix_spec = pl.BlockSpec((tm, GROUP_W), lambda i: (i, 0))
    return pl.pallas_call(
        _out_proj_kernel,
        out_shape=jax.ShapeDtypeStruct((t, d), F32),
        grid=(t // tm,),
        in_specs=[pl.BlockSpec((tm, d), lambda i: (i, 0)), mix_spec, mix_spec, mix_spec, mix_spec,
                  pl.BlockSpec(w.shape, lambda i: (0, 0))],
        out_specs=pl.BlockSpec((tm, d), lambda i: (i, 0)),
        compiler_params=_params("parallel"),
        name="out_proj",
    )(x, *mixes, w)


def _ffn_kernel(x_ref, g_ref, wg_ref, wu_ref, wd_ref, gf_ref, o_ref, h_ref, acc_ref, *, final_norm):
    j = pl.program_id(1)

    @pl.when(j == 0)
    def _():
        h_ref[...] = _rms(x_ref[...], g_ref[...]).astype(BF16)
        acc_ref[...] = jnp.zeros_like(acc_ref)

    h = h_ref[...]
    gate = _dot(h, wg_ref[...])
    up = _dot(h, wu_ref[...])
    act = (gate * jax.nn.sigmoid(gate) * up).astype(BF16)
    acc_ref[...] += _dot(act, wd_ref[...])

    @pl.when(j == pl.num_programs(1) - 1)
    def _():
        y = x_ref[...] + acc_ref[...]
        if final_norm:
            y = _rms(y, gf_ref[...])
        o_ref[...] = y


def _ffn(x, g, wg, wu, wd, g_final, final_norm):
    t, d = x.shape
    f = wg.shape[1]
    tm = _pick(t, _SMALL_ROW_TILES)
    tf = _pick(f, (512, 256, 128))
    return pl.pallas_call(
        functools.partial(_ffn_kernel, final_norm=final_norm),
        out_shape=jax.ShapeDtypeStruct((t, d), F32),
        grid=(t // tm, f // tf),
        in_specs=[pl.BlockSpec((tm, d), lambda i, j: (i, 0)),
                  pl.BlockSpec((1, d), lambda i, j: (0, 0)),
                  pl.BlockSpec((d, tf), lambda i, j: (0, j)),
                  pl.BlockSpec((d, tf), lambda i, j: (0, j)),
                  pl.BlockSpec((tf, d), lambda i, j: (j, 0)),
                  pl.BlockSpec((1, d), lambda i, j: (0, 0))],
        out_specs=pl.BlockSpec((tm, d), lambda i, j: (i, 0)),
        scratch_shapes=[pltpu.VMEM((tm, d), BF16), pltpu.VMEM((tm, d), F32)],
        compiler_params=_params("parallel", "arbitrary"),
        name="ffn",
    )(x, g, wg, wu, wd, g_final)


def _lambda(lam_ref, lam_init):
    lp = lam_ref[...]
    s01 = jnp.sum(lp[0:1, :] * lp[1:2, :], axis=1, keepdims=True)
    s23 = jnp.sum(lp[2:3, :] * lp[3:4, :], axis=1, keepdims=True)
    return jnp.exp(s01) - jnp.exp(s23) + lam_init


def _split_maps(q):
    lane = lax.broadcasted_iota(jnp.int32, q.shape, 1)
    return jnp.concatenate([jnp.where(lane < DH_A, q, 0.0), jnp.where(lane >= DH_A, q, 0.0)], axis=0)


def _softmax_update(s, m_ref, l_ref):
    m_prev = m_ref[...]
    m_new = jnp.maximum(m_prev, jnp.max(s, axis=1, keepdims=True))
    alpha = jnp.exp2(m_prev - m_new)
    p = jnp.exp2(s - m_new)
    l_ref[...] = alpha * l_ref[...] + jnp.sum(p, axis=1, keepdims=True)
    m_ref[...] = m_new
    return p, alpha


def _flash_loop(segments, n_total, scores_fn, values_fn, m_ref, acc_ref, s_ref, p_ref, a_ref):
    reps = s_ref.shape[1] // V7X_LANES
    m_ref[...] = jnp.full_like(m_ref, M_INIT)
    acc_ref[...] = jnp.zeros_like(acc_ref)
    p_ref[...] = jnp.zeros_like(p_ref)
    a_ref[...] = jnp.ones_like(a_ref)
    s_ref[...] = scores_fn(0)

    def finish(j):
        acc_ref[:, :DH] = a_ref[...] * acc_ref[:, :DH] + _dot(p_ref[...], values_fn(j))

    def make_body(keep_fn):
        def body(j, carry):
            finish(jnp.maximum(j - 1, 0))
            s = s_ref[...]
            s_ref[...] = scores_fn(jnp.minimum(j + 1, n_total - 1))
            if keep_fn is not None:
                s = jnp.where(keep_fn(j), s, NEG_INF)
            m_prev = m_ref[...]
            m_new = jnp.maximum(m_prev, jnp.max(s, axis=1, keepdims=True))
            alpha = jnp.exp2(m_prev - m_new)
            p = jnp.exp2(s - jnp.concatenate([m_new] * reps, axis=1))
            acc_ref[:, DH:] = alpha * acc_ref[:, DH:] + jnp.sum(p, axis=1, keepdims=True)
            a_ref[...] = alpha
            m_ref[...] = m_new
            p_ref[...] = p.astype(BF16)
            return carry
        return body

    for lo, hi, keep_fn in segments:
        lax.fori_loop(lo, hi, make_body(keep_fn), 0)
    finish(n_total - 1)
    acc = acc_ref[...]
    return acc[:, :DH], acc[:, DH:]


def _flash_scratch(rows, tk):
    return [pltpu.VMEM((rows, V7X_LANES), F32), pltpu.VMEM((rows, 2 * DH), F32),
            pltpu.VMEM((rows, tk), F32), pltpu.VMEM((rows, tk), BF16),
            pltpu.VMEM((rows, V7X_LANES), F32)]


def _softplus(z):
    return jnp.maximum(z, 0.0) + jnp.log(1.0 + jnp.exp(-jnp.abs(z)))


def _later_sum(lk, upper):
    hi = lk.astype(BF16)
    lo = (lk - hi.astype(F32)).astype(BF16)
    return _dot(hi, upper) + _dot(lo, upper)


def _suffix_matrix(n):
    j = lax.broadcasted_iota(jnp.int32, (n, n), 0)
    s = lax.broadcasted_iota(jnp.int32, (n, n), 1)
    return (j > s).astype(BF16)


def _attn_a_kernel(lam_ref, gsub_ref, q_ref, k_ref, v_ref, o_ref, *scratch, tq, tk, lam_init):
    qi = pl.program_id(1)
    qq = _split_maps(q_ref[...] * (DH_A ** -0.5 * LOG2E)).astype(BF16)

    def block(ref, j):
        return ref[pl.ds(pl.multiple_of(j * tk, tk), tk), :]

    def causal(j):
        row = lax.broadcasted_iota(jnp.int32, (2 * tq, tk), 0)
        col = lax.broadcasted_iota(jnp.int32, (2 * tq, tk), 1)
        q_pos = qi * tq + jnp.where(row >= tq, row - tq, row)
        return j * tk + col <= q_pos

    n_full = (qi * tq) // tk
    n_total = (qi * tq + tq - 1) // tk + 1
    acc, l = _flash_loop([(0, n_full, None), (n_full, n_total, causal)], n_total,
                         lambda j: _dot_nt(qq, block(k_ref, j)), lambda j: block(v_ref, j), *scratch)
    lam = _lambda(lam_ref, lam_init)
    o = acc[:tq] / l[:tq] - lam * (acc[tq:] / l[tq:])
    o_ref[...] = (_rms(o, gsub_ref[...]) * (1.0 - lam_init)).astype(o_ref.dtype)


def _attn_a(proj, proj_b, s, lam_a, g_sub, lam_init):
    tq = _pick(s, (256, 128))
    tk = _pick(s, (512, 256, 128))
    return pl.pallas_call(
        functools.partial(_attn_a_kernel, tq=tq, tk=tk, lam_init=lam_init),
        out_shape=jax.ShapeDtypeStruct((s, GROUP_W), BF16),
        grid=(N_HEADS, s // tq),
        in_specs=[pl.BlockSpec(lam_a.shape, lambda h, i: (0, 0)),
                  pl.BlockSpec((1, DH), lambda h, i: (0, 0)),
                  pl.BlockSpec((tq, DH), lambda h, i: (i, COL_QA * N_HEADS + h)),
                  pl.BlockSpec((s, DH), lambda h, i: (0, COL_KA * N_HEADS + h)),
                  pl.BlockSpec((s, DH), lambda h, i: (0, COL_VA * N_HEADS + h))],
        out_specs=pl.BlockSpec((tq, DH), lambda h, i: (i, h)),
        scratch_shapes=_flash_scratch(2 * tq, tk),
        compiler_params=_params("parallel", "arbitrary"),
        name="prompt_diff_attn",
    )(lam_a, g_sub, proj, proj_b, proj_b)


def _block_mean_kernel(k_ref, o_ref, *, n_blocks):
    o_ref[...] = jnp.zeros_like(o_ref)

    def body(b, carry):
        rows = k_ref[pl.ds(pl.multiple_of(b * MOBA_BLOCK, MOBA_BLOCK), MOBA_BLOCK), :]
        o_ref[pl.ds(b, 1), :] = jnp.mean(rows, axis=0, keepdims=True)
        return carry

    lax.fori_loop(0, n_blocks, body, 0)


def _block_means(proj, s, n_pad):
    return pl.pallas_call(
        functools.partial(_block_mean_kernel, n_blocks=s // MOBA_BLOCK),
        out_shape=jax.ShapeDtypeStruct((N_HEADS, n_pad, DH), F32),
        grid=(N_HEADS,),
        in_specs=[pl.BlockSpec((s, DH), lambda h: (0, COL_KB * N_HEADS + h))],
        out_specs=pl.BlockSpec((None, n_pad, DH), lambda h: (h, 0, 0)),
        compiler_params=_params("parallel"),
        name="moba_block_means",
    )(proj)


def _moba_select(gates, own):
    lane = lax.broadcasted_iota(jnp.int32, gates.shape, 1)
    g = jnp.where(lane < own, gates, NEG_INF)
    sel = jnp.where(lane == own, 1.0, 0.0)
    for _ in range(MOBA_TOPK):
        mx = jnp.max(g, axis=1, keepdims=True)
        pick = jnp.min(jnp.where(g == mx, lane, gates.shape[1]), axis=1, keepdims=True)
        hit = jnp.logical_and(lane == pick, mx > NEG_INF)
        sel = jnp.where(hit, 1.0, sel)
        g = jnp.where(lane == pick, NEG_INF, g)
    return sel


def _attn_b_kernel(q_ref, kmean_ref, k_ref, v_ref, o_ref, *scratch, t):
    qi = pl.program_id(1)
    per_tile = t // MOBA_BLOCK
    q_raw = q_ref[...]
    gates = lax.dot_general(q_raw, kmean_ref[...], _NT, precision=lax.Precision.HIGHEST,
                            preferred_element_type=F32)
    row1 = lax.broadcasted_iota(jnp.int32, (t, 1), 0)
    own = qi * per_tile + _shift_div(row1, MOBA_BLOCK)
    sel = _moba_select(gates, own)
    q = jnp.concatenate([(q_raw * (DH ** -0.5 * LOG2E)).astype(BF16),
                         jnp.where(sel > 0.5, 0.0, MASK_BIAS).astype(BF16)], axis=1)

    def block(ref, j):
        return ref[pl.ds(pl.multiple_of(j * t, t), t), :]

    def scores(j):
        key = lax.broadcasted_iota(jnp.int32, (t, sel.shape[1]), 0)
        blk = lax.broadcasted_iota(jnp.int32, (t, sel.shape[1]), 1)
        onehot = (blk == j * per_tile + _shift_div(key, MOBA_BLOCK)).astype(BF16)
        return _dot_nt(q, jnp.concatenate([block(k_ref, j), onehot], axis=1))

    def causal(j):
        row = lax.broadcasted_iota(jnp.int32, (t, t), 0)
        col = lax.broadcasted_iota(jnp.int32, (t, t), 1)
        return col <= row

    acc, l = _flash_loop([(0, qi, None), (qi, qi + 1, causal)], qi + 1,
                         scores, lambda j: block(v_ref, j), *scratch)
    o_ref[...] = (acc / l).astype(o_ref.dtype)


def _attn_b(proj, proj_b, s):
    t = _pick(s, (512, 256))
    n_pad = -(-(s // MOBA_BLOCK) // V7X_LANES) * V7X_LANES
    kmean = _block_means(proj, s, n_pad)
    return pl.pallas_call(
        functools.partial(_attn_b_kernel, t=t),
        out_shape=jax.ShapeDtypeStruct((s, GROUP_W), BF16),
        grid=(N_HEADS, s // t),
        in_specs=[pl.BlockSpec((t, DH), lambda h, i: (i, COL_QB * N_HEADS + h)),
                  pl.BlockSpec((None, n_pad, DH), lambda h, i: (h, 0, 0)),
                  pl.BlockSpec((s, DH), lambda h, i: (0, COL_KB * N_HEADS + h)),
                  pl.BlockSpec((s, DH), lambda h, i: (0, COL_VB * N_HEADS + h))],
        out_specs=pl.BlockSpec((t, DH), lambda h, i: (i, h)),
        scratch_shapes=_flash_scratch(t, t),
        compiler_params=_params("parallel", "arbitrary"),
        name="prompt_moba",
    )(proj, kmean, proj_b, proj_b)


def _attn_d_kernel(q_ref, k_ref, v_ref, o_ref, r_ref, acc_ref, *, tq, tk):
    qi = pl.program_id(1)
    q = (q_ref[...] * (DH ** -0.5)).astype(BF16)
    r_ref[...] = jnp.zeros_like(r_ref)
    acc_ref[...] = jnp.zeros_like(acc_ref)

    def cond(carry):
        kb, r_max = carry
        return jnp.logical_and(kb >= 0, r_max > SB_DEAD)

    def body(carry):
        kb, _ = carry
        start = pl.multiple_of(kb * tk, tk)
        z = _dot_nt(q, k_ref[pl.ds(start, tk), :])
        row = lax.broadcasted_iota(jnp.int32, (tq, tk), 0)
        col = lax.broadcasted_iota(jnp.int32, (tq, tk), 1)
        strict = start + col < qi * tq + row
        sp = _softplus(z)
        lk = jnp.where(strict, -sp, 0.0)
        later = _later_sum(lk, _suffix_matrix(tk))
        r = r_ref[...]
        w = jnp.where(strict, jnp.exp(z - sp + later + r), 0.0)
        acc_ref[...] += _dot(w.astype(BF16), v_ref[pl.ds(start, tk), :])
        r_new = r + jnp.sum(lk, axis=1, keepdims=True)
        r_ref[...] = r_new
        return kb - 1, jnp.max(r_new)

    lax.while_loop(cond, body, ((qi * tq + tq - 1) // tk, jnp.float32(0.0)))
    o_ref[...] = acc_ref[...].astype(o_ref.dtype)


def _attn_d(proj, proj_b, s):
    tq = _pick(s, (256, 128))
    tk = _pick(s, (256, 128))
    return pl.pallas_call(
        functools.partial(_attn_d_kernel, tq=tq, tk=tk),
        out_shape=jax.ShapeDtypeStruct((s, GROUP_W), BF16),
        grid=(N_HEADS, s // tq),
        in_specs=[pl.BlockSpec((tq, DH), lambda h, i: (i, COL_QD * N_HEADS + h)),
                  pl.BlockSpec((s, DH), lambda h, i: (0, COL_KD * N_HEADS + h)),
                  pl.BlockSpec((s, DH), lambda h, i: (0, COL_VD * N_HEADS + h))],
        out_specs=pl.BlockSpec((tq, DH), lambda h, i: (i, h)),
        scratch_shapes=[pltpu.VMEM((tq, 1), F32), pltpu.VMEM((tq, DH), F32)],
        compiler_params=_params("parallel", "arbitrary"),
        name="prompt_stickbreak",
    )(proj, proj_b, proj_b)


def _conv_kernel(*refs, tt, has_prev):
    if has_prev:
        cur_ref, prev_ref, cbuf_ref, bglu_ref, wdw_ref, bdw_ref, gn_ref, bn_ref, o_ref, nbuf_ref, ubuf_ref = refs
    else:
        cur_ref, cbuf_ref, bglu_ref, wdw_ref, bdw_ref, gn_ref, bn_ref, o_ref, nbuf_ref, ubuf_ref = refs
    i = pl.program_id(1)
    c = GROUP_W

    def glu(x):
        x = x + bglu_ref[...]
        return x[:, :c] * jax.nn.sigmoid(x[:, c:])

    ubuf_ref[CONV_HALO:CONV_HALO + tt, :] = glu(cur_ref[...])

    @pl.when(i == 0)
    def _():
        ubuf_ref[CONV_HALO - (CONV_W - 1):CONV_HALO, :] = cbuf_ref[...]

    if has_prev:
        @pl.when(i > 0)
        def _():
            ubuf_ref[0:CONV_HALO, :] = glu(prev_ref[...])

    off = CONV_HALO - (CONV_W - 1)
    y = jnp.zeros((tt, c), F32) + bdw_ref[...]
    for j in range(CONV_W):
        y = y + ubuf_ref[off + j:off + j + tt, :] * wdw_ref[j:j + 1, :]

    gw = c // N_NORM_GROUPS
    for gi in range(N_NORM_GROUPS):
        seg = y[:, gi * gw:(gi + 1) * gw]
        mu = jnp.mean(seg, axis=-1, keepdims=True)
        var = jnp.mean(jnp.square(seg - mu), axis=-1, keepdims=True)
        yn = (seg - mu) * lax.rsqrt(var + EPS)
        yn = yn * gn_ref[:, gi * gw:(gi + 1) * gw] + bn_ref[:, gi * gw:(gi + 1) * gw]
        o_ref[:, gi * gw:(gi + 1) * gw] = (yn * jax.nn.sigmoid(yn)).astype(o_ref.dtype)

    @pl.when(i == pl.num_programs(1) - 1)
    def _():
        nbuf_ref[...] = ubuf_ref[CONV_HALO + tt - (CONV_W - 1):CONV_HALO + tt, :]


def _conv_group(proj, batch, s, conv_buf, b_glu, w_dw, b_dw, g_norm, b_norm, out_dtype):
    tt = _pick(s, _SMALL_ROW_TILES)
    nt = s // tt
    has_prev = nt > 1
    assert tt + CONV_HALO >= CONV_W - 1 and (not has_prev or tt % CONV_HALO == 0)
    c = GROUP_W
    glu_col = COL_GLU_A // 2
    in_specs = [pl.BlockSpec((tt, 2 * c), lambda b, i: (b * nt + i, glu_col))]
    args = [proj]
    if has_prev:
        per = tt // CONV_HALO
        in_specs.append(pl.BlockSpec((CONV_HALO, 2 * c),
                                     lambda b, i: (jnp.maximum((b * nt + i) * per - 1, 0), glu_col)))
        args.append(proj)
    vec = lambda n: pl.BlockSpec((1, n), lambda b, i: (0, 0))
    in_specs += [pl.BlockSpec((None, CONV_W - 1, c), lambda b, i: (b, 0, 0)),
                 vec(2 * c), pl.BlockSpec((CONV_W, c), lambda b, i: (0, 0)), vec(c), vec(c), vec(c)]
    args += [conv_buf, b_glu, w_dw, b_dw, g_norm, b_norm]
    return pl.pallas_call(
        functools.partial(_conv_kernel, tt=tt, has_prev=has_prev),
        out_shape=(jax.ShapeDtypeStruct((batch * s, c), out_dtype),
                   jax.ShapeDtypeStruct((batch, CONV_W - 1, c), F32)),
        grid=(batch, nt),
        in_specs=in_specs,
        out_specs=(pl.BlockSpec((tt, c), lambda b, i: (b * nt + i, 0)),
                   pl.BlockSpec((None, CONV_W - 1, c), lambda b, i: (b, 0, 0))),
        scratch_shapes=[pltpu.VMEM((CONV_HALO + tt, c), F32)],
        compiler_params=_params("parallel", "arbitrary"),
        name="conv_module",
    )(*args)


def _page_heads(ref, page, dtype):
    return [ref[pl.ds(h, page, stride=N_HEADS), :].astype(dtype) for h in range(N_HEADS)]


def _pad_heads(pad_ref, new_ref, dtype):
    pad_ref[...] = jnp.zeros_like(pad_ref)
    pad_ref[0:new_ref.shape[0], :] = new_ref[...]
    return [pad_ref[:, h * DH:(h + 1) * DH].astype(dtype) for h in range(N_HEADS)]


def _head_rows(fn):
    return jnp.concatenate([fn(h) for h in range(N_HEADS)], axis=0)


def _new_key_index(shape, nq):
    row = lax.broadcasted_iota(jnp.int32, shape, 0)
    col = lax.broadcasted_iota(jnp.int32, shape, 1)
    return col, row & (nq - 1)


def _dec_a_kernel(pt_ref, lam_ref, gsub_ref, q_ref, kn_ref, vn_ref, *rest, n_pg, nq, page, lam_init):
    del pt_ref
    k_pages, v_pages = rest[:n_pg], rest[n_pg:2 * n_pg]
    o_ref, qh_ref, kpad_ref, vpad_ref, m_ref, l_ref, acc_ref = rest[2 * n_pg:]
    g = pl.program_id(1)
    rh = 2 * nq

    def reduce(s, pv_fn):
        p, alpha = _softmax_update(s, m_ref, l_ref)
        pv = _head_rows(lambda h: pv_fn(h, p[h * rh:(h + 1) * rh].astype(BF16)))
        acc_ref[...] = alpha * acc_ref[...] + pv

    @pl.when(g == 0)
    def _():
        for h in range(N_HEADS):
            qh = q_ref[:, h * DH:(h + 1) * DH] * (DH_A ** -0.5 * LOG2E)
            qh_ref[h] = _split_maps(qh).astype(BF16)
        m_ref[...] = jnp.full_like(m_ref, M_INIT)
        l_ref[...] = jnp.zeros_like(l_ref)
        acc_ref[...] = jnp.zeros_like(acc_ref)
        kh = _pad_heads(kpad_ref, kn_ref, BF16)
        vh = _pad_heads(vpad_ref, vn_ref, BF16)
        s = _head_rows(lambda h: _dot_nt(qh_ref[h], kh[h]))
        key, qry = _new_key_index(s.shape, nq)
        reduce(jnp.where(key <= qry, s, NEG_INF), lambda h, ph: _dot(ph, vh[h]))

    kh = [_page_heads(r, page, BF16) for r in k_pages]
    vh = [_page_heads(r, page, BF16) for r in v_pages]
    keys_of = lambda pages, h: jnp.concatenate([pages[t][h] for t in range(n_pg)], axis=0)
    s = _head_rows(lambda h: _dot_nt(qh_ref[h], keys_of(kh, h)))
    reduce(s, lambda h, ph: _dot(ph, keys_of(vh, h)))

    @pl.when(g == pl.num_programs(1) - 1)
    def _():
        acc = acc_ref[...]
        l = l_ref[...]
        lam = _lambda(lam_ref, lam_init)
        for h in range(N_HEADS):
            r1, r2 = h * rh, h * rh + nq
            o = acc[r1:r1 + nq] / l[r1:r1 + nq] - lam * (acc[r2:r2 + nq] / l[r2:r2 + nq])
            o_ref[:, h * DH:(h + 1) * DH] = _rms(o, gsub_ref[...]) * (1.0 - lam_init)


def _stickbreak_pages(qh_ref, k_pages, v_pages, r_ref, acc_ref, rmax_ref, nq, page):
    def step(kh, vh, strict, r):
        z = _head_rows(lambda h: _dot_nt(qh_ref[h], kh[h]))
        sp = _softplus(z)
        lk = -sp if strict is None else jnp.where(strict(z.shape), -sp, 0.0)
        later = _later_sum(lk, _suffix_matrix(page))
        w = jnp.exp(z - sp + later + r)
        if strict is not None:
            w = jnp.where(strict(z.shape), w, 0.0)
        acc_ref[...] += _head_rows(lambda h: _dot(w[h * nq:(h + 1) * nq].astype(BF16), vh[h]))
        r_new = r + jnp.sum(lk, axis=1, keepdims=True)
        r_ref[...] = r_new
        rmax_ref[0] = jnp.max(r_new)

    def run_pages():
        for k_ref, v_ref in zip(k_pages, v_pages):
            @pl.when(rmax_ref[0] > SB_DEAD)
            def _():
                step(_page_heads(k_ref, page, BF16), _page_heads(v_ref, page, BF16), None, r_ref[...])

    return step, run_pages


def _dec_d_head_kernel(pt_ref, q_ref, kn_ref, vn_ref, *rest, n_pg, nq, page):
    del pt_ref
    k_pages, v_pages = rest[:n_pg], rest[n_pg:2 * n_pg]
    acc_out, r_out, qh_ref, kpad_ref, vpad_ref, r_ref, acc_ref, rmax_ref = rest[2 * n_pg:]
    step, run_pages = _stickbreak_pages(qh_ref, k_pages, v_pages, r_ref, acc_ref, rmax_ref, nq, page)
    for h in range(N_HEADS):
        qh_ref[h] = (q_ref[:, h * DH:(h + 1) * DH] * (DH ** -0.5)).astype(BF16)
    acc_ref[...] = jnp.zeros_like(acc_ref)

    def strict(shape):
        key, qry = _new_key_index(shape, nq)
        return key < qry

    step(_pad_heads(kpad_ref, kn_ref, BF16), _pad_heads(vpad_ref, vn_ref, BF16), strict,
         jnp.zeros(r_ref.shape, F32))
    run_pages()
    acc_out[...] = acc_ref[...]
    r_out[...] = jnp.broadcast_to(r_ref[...], r_out.shape)


def _dec_d_tail_kernel(pt_ref, dead_ref, q_ref, acc_in, r_in, *rest, n_pg, nq, page):
    del pt_ref
    k_pages, v_pages = rest[:n_pg], rest[n_pg:2 * n_pg]
    o_ref, qh_ref, r_ref, acc_ref, rmax_ref = rest[2 * n_pg:]
    g = pl.program_id(1)
    _, run_pages = _stickbreak_pages(qh_ref, k_pages, v_pages, r_ref, acc_ref, rmax_ref, nq, page)

    @pl.when(g == 0)
    def _():
        for h in range(N_HEADS):
            qh_ref[h] = (q_ref[:, h * DH:(h + 1) * DH] * (DH ** -0.5)).astype(BF16)
        acc_ref[...] = acc_in[...]
        r = r_in[:, 0:1]
        r_ref[...] = r
        rmax_ref[0] = jnp.where(dead_ref[pl.program_id(0)] == 1, NEG_INF, jnp.max(r))

    run_pages()

    @pl.when(g == pl.num_programs(1) - 1)
    def _():
        acc = acc_ref[...]
        for h in range(N_HEADS):
            o_ref[:, h * DH:(h + 1) * DH] = acc[h * nq:(h + 1) * nq]


def _dec_b_kernel(pt_ref, q_ref, kn_ref, vn_ref, *rest, n_pg, nq, page, n_blocks):
    del pt_ref
    k_pages, v_pages = rest[:n_pg], rest[n_pg:2 * n_pg]
    o_ref, qf_ref, qh_ref, kpad_ref, vpad_ref, g_ref, m_ref, l_ref, acc_ref = rest[2 * n_pg:]
    g = pl.program_id(1)
    per_block = MOBA_BLOCK // page
    lane = lax.broadcasted_iota(jnp.int32, g_ref.shape, 1)

    @pl.when(g == 0)
    def _():
        for h in range(N_HEADS):
            qh = q_ref[:, h * DH:(h + 1) * DH]
            qf_ref[h] = qh
            qh_ref[h] = (qh * (DH ** -0.5 * LOG2E)).astype(BF16)
        g_ref[...] = jnp.full_like(g_ref, NEG_INF)
        m_ref[...] = jnp.full_like(m_ref, M_INIT)
        l_ref[...] = jnp.zeros_like(l_ref)

    gates, maxes, sums = g_ref[...], m_ref[...], l_ref[...]
    kf = [_page_heads(r, page, F32) for r in k_pages]
    vh = [_page_heads(r, page, BF16) for r in v_pages]
    s_all = _head_rows(lambda h: _dot_nt(
        qh_ref[h], jnp.concatenate([kf[t][h].astype(BF16) for t in range(n_pg)], axis=0)))
    for blk in range(n_pg // per_block):
        pages = range(blk * per_block, (blk + 1) * per_block)

        def gate_of(h):
            k_mean = sum(jnp.sum(kf[t][h], axis=0, keepdims=True) for t in pages)
            return jnp.sum(qf_ref[h] * (k_mean * (1.0 / MOBA_BLOCK)), axis=1, keepdims=True)

        gate = _head_rows(gate_of)
        s = s_all[:, blk * MOBA_BLOCK:(blk + 1) * MOBA_BLOCK]
        m_b = jnp.max(s, axis=1, keepdims=True)
        p = jnp.exp2(s - m_b)
        l_b = jnp.sum(p, axis=1, keepdims=True)
        acc_b = _head_rows(lambda h: _dot(p[h * nq:(h + 1) * nq].astype(BF16),
                                          jnp.concatenate([vh[t][h] for t in pages], axis=0)))
        b_idx = g * (n_pg // per_block) + blk
        acc_ref[b_idx] = acc_b
        here = lane == b_idx
        gates = jnp.where(here, gate, gates)
        maxes = jnp.where(here, m_b, maxes)
        sums = jnp.where(here, l_b, sums)
    g_ref[...], m_ref[...], l_ref[...] = gates, maxes, sums

    @pl.when(g == pl.num_programs(1) - 1)
    def _():
        kh = _pad_heads(kpad_ref, kn_ref, BF16)
        vh = _pad_heads(vpad_ref, vn_ref, BF16)
        s = _head_rows(lambda h: _dot_nt(qh_ref[h], kh[h]))
        key, qry = _new_key_index(s.shape, nq)
        s = jnp.where(key <= qry, s, NEG_INF)
        m_o = jnp.max(s, axis=1, keepdims=True)
        p = jnp.exp2(s - m_o)
        chosen = jnp.logical_and(_moba_select(gates, n_blocks) > 0.5, lane < n_blocks)
        m_all = jnp.maximum(m_o, jnp.max(jnp.where(chosen, maxes, NEG_INF), axis=1, keepdims=True))
        weight = jnp.where(chosen, jnp.exp2(maxes - m_all), 0.0)
        w_o = jnp.exp2(m_o - m_all)
        den = w_o * jnp.sum(p, axis=1, keepdims=True) + jnp.sum(weight * sums, axis=1, keepdims=True)

        num = w_o * _head_rows(lambda h: _dot(p[h * nq:(h + 1) * nq].astype(BF16), vh[h]))
        for b in range(n_blocks):
            w_b = jnp.sum(jnp.where(lane == b, weight, 0.0), axis=1, keepdims=True)
            num = num + w_b * acc_ref[b]
        out = num / den
        for h in range(N_HEADS):
            o_ref[:, h * DH:(h + 1) * DH] = out[h * nq:(h + 1) * nq]


def _dec_attention(kind, proj, k_cache, v_cache, page_table, li, nq, cols, extra=(), **kw):
    n_seq, n_pages = page_table.shape
    page = k_cache.shape[2] // N_HEADS
    n_pg = _pick(n_pages, (16, 8, 4, 2))
    steps = n_pages // n_pg
    col_q, col_k, col_v = cols
    w = GROUP_W
    assert nq % V7X_SUBLANES == 0 and nq & (nq - 1) == 0 and nq <= page
    assert (n_pages * page) % MOBA_BLOCK == 0 and MOBA_BLOCK % page == 0

    def page_spec(t):
        return pl.BlockSpec((None, None, page * N_HEADS, DH),
                            lambda b, g, pt: (li, pt[b, g * n_pg + t], 0, 0))

    small = lambda a: pl.BlockSpec(a.shape, lambda b, g, pt: (0,) * a.ndim)
    row_spec = lambda col: pl.BlockSpec((nq, w), lambda b, g, pt: (b, col))
    in_specs = ([small(a) for a in extra] + [row_spec(col_q), row_spec(col_k), row_spec(col_v)]
                + [page_spec(t) for t in range(n_pg)] * 2)
    args = list(extra) + [proj, proj, proj] + [k_cache] * n_pg + [v_cache] * n_pg

    pad = [pltpu.VMEM((page, w), F32), pltpu.VMEM((page, w), F32)]
    stat = lambda rows: pltpu.VMEM((rows, 1), F32)
    if kind == "a":
        rows = 2 * N_HEADS * nq
        body = functools.partial(_dec_a_kernel, n_pg=n_pg, nq=nq, page=page, **kw)
        scratch = [pltpu.VMEM((N_HEADS, 2 * nq, DH), BF16)] + pad + [
            stat(rows), stat(rows), pltpu.VMEM((rows, DH), F32)]
    else:
        rows = N_HEADS * nq
        n_blocks = n_pages * page // MOBA_BLOCK
        assert n_blocks <= V7X_LANES
        body = functools.partial(_dec_b_kernel, n_pg=n_pg, nq=nq, page=page, n_blocks=n_blocks)
        lanes = lambda: pltpu.VMEM((rows, V7X_LANES), F32)
        scratch = [pltpu.VMEM((N_HEADS, nq, DH), F32), pltpu.VMEM((N_HEADS, nq, DH), BF16)] + pad + [
            lanes(), lanes(), lanes(), pltpu.VMEM((n_blocks, rows, DH), F32)]

    return pl.pallas_call(
        body,
        out_shape=jax.ShapeDtypeStruct((n_seq * nq, w), F32),
        grid_spec=pltpu.PrefetchScalarGridSpec(
            num_scalar_prefetch=1,
            grid=(n_seq, steps),
            in_specs=in_specs,
            out_specs=pl.BlockSpec((nq, w), lambda b, g, pt: (b, 0)),
            scratch_shapes=scratch),
        compiler_params=_params("parallel", "arbitrary"),
        name="sample_attn_" + kind,
    )(page_table, *args)


def _dec_stickbreak(proj, k_cache, v_cache, page_table, li, nq):
    n_seq, n_pages = page_table.shape
    page = k_cache.shape[2] // N_HEADS
    n_head = 8 if n_pages >= 16 else n_pages // 2
    n_pg = _pick(n_pages - n_head, (8, 4, 2, 1))
    steps = (n_pages - n_head) // n_pg
    rows, w = N_HEADS * nq, GROUP_W
    assert 0 < n_head < n_pages
    page_block = (None, None, page * N_HEADS, DH)
    qh = pltpu.VMEM((N_HEADS, nq, DH), BF16)
    state = [pltpu.VMEM((rows, 1), F32), pltpu.VMEM((rows, DH), F32), pltpu.SMEM((1,), F32)]

    def head_page(t):
        return pl.BlockSpec(page_block, lambda b, pt: (li, pt[b, n_pages - 1 - t], 0, 0))

    row_spec = lambda col: pl.BlockSpec((nq, w), lambda b, pt: (b, col))
    acc0, r0 = pl.pallas_call(
        functools.partial(_dec_d_head_kernel, n_pg=n_head, nq=nq, page=page),
        out_shape=[jax.ShapeDtypeStruct((n_seq * rows, DH), F32),
                   jax.ShapeDtypeStruct((n_seq * rows, V7X_LANES), F32)],
        grid_spec=pltpu.PrefetchScalarGridSpec(
            num_scalar_prefetch=1,
            grid=(n_seq,),
            in_specs=[row_spec(COL_QD), row_spec(COL_KD), row_spec(COL_VD)]
            + [head_page(t) for t in range(n_head)] * 2,
            out_specs=[pl.BlockSpec((rows, DH), lambda b, pt: (b, 0)),
                       pl.BlockSpec((rows, V7X_LANES), lambda b, pt: (b, 0))],
            scratch_shapes=[qh, pltpu.VMEM((page, w), F32), pltpu.VMEM((page, w), F32)] + state),
        compiler_params=_params("parallel"),
        name="sample_stickbreak_head",
    )(page_table, proj, proj, proj, *([k_cache] * n_head), *([v_cache] * n_head))

    dead = (jnp.max(r0.reshape(n_seq, -1), axis=1) < SB_DEAD).astype(jnp.int32)

    def tail_page(t):
        def imap(b, g, pt, dd):
            idx = n_pages - 1 - n_head - (g * n_pg + t)
            return (li, jnp.where(dd[b] == 1, 0, pt[b, idx]), 0, 0)
        return pl.BlockSpec(page_block, imap)

    tail = pl.pallas_call(
        functools.partial(_dec_d_tail_kernel, n_pg=n_pg, nq=nq, page=page),
        out_shape=jax.ShapeDtypeStruct((n_seq * nq, w), F32),
        grid_spec=pltpu.PrefetchScalarGridSpec(
            num_scalar_prefetch=2,
            grid=(n_seq, steps),
            in_specs=[pl.BlockSpec((nq, w), lambda b, g, pt, dd: (b, COL_QD)),
                      pl.BlockSpec((rows, DH), lambda b, g, pt, dd: (b, 0)),
                      pl.BlockSpec((rows, V7X_LANES), lambda b, g, pt, dd: (b, 0))]
            + [tail_page(t) for t in range(n_pg)] * 2,
            out_specs=pl.BlockSpec((nq, w), lambda b, g, pt, dd: (b, 0)),
            scratch_shapes=[qh] + state),
        compiler_params=_params("parallel", "arbitrary"),
        name="sample_stickbreak_tail",
    )

    def run_tail():
        return tail(page_table, dead, proj, acc0, r0, *([k_cache] * n_pg), *([v_cache] * n_pg))

    def all_dead():
        return acc0.reshape(n_seq, N_HEADS, nq, DH).transpose(0, 2, 1, 3).reshape(n_seq * nq, w)

    return lax.cond(jnp.any(dead == 0), run_tail, all_dead)


KV_COLS = (COL_KA, COL_VA, COL_KB, COL_VB, COL_KD, COL_VD)


def _new_rows_kernel(*refs, n_layers, tm):
    n_in = n_layers * len(KV_COLS)
    ins, outs = refs[:n_in], refs[n_in:]
    for li in range(n_layers):
        @pl.when(pl.program_id(0) == li)
        def _():
            for gi, o_ref in enumerate(outs):
                src = ins[li * len(KV_COLS) + gi]
                for h in range(N_HEADS):
                    o_ref[pl.ds(h, tm, stride=N_HEADS), :] = src[:, h * DH:(h + 1) * DH]


def _new_rows(projs):
    n_layers = len(projs)
    t = projs[0].shape[0]
    tm = _pick(t, _SMALL_ROW_TILES)
    nt = t // tm

    def in_spec(li, col):
        def imap(d, i):
            return (jnp.where(d == li, i, jnp.where(d > li, nt - 1, 0)), col)
        return pl.BlockSpec((tm, GROUP_W), imap)

    in_specs = [in_spec(li, col) for li in range(n_layers) for col in KV_COLS]
    args = [projs[li] for li in range(n_layers) for _ in KV_COLS]
    out_shape = [jax.ShapeDtypeStruct((n_layers, t * N_HEADS, DH), F32)] * len(KV_COLS)
    out_specs = [pl.BlockSpec((None, tm * N_HEADS, DH), lambda d, i: (d, i, 0))] * len(KV_COLS)
    return pl.pallas_call(
        functools.partial(_new_rows_kernel, n_layers=n_layers, tm=tm),
        out_shape=out_shape,
        grid=(n_layers, nt),
        in_specs=in_specs,
        out_specs=out_specs,
        compiler_params=_params("arbitrary", "arbitrary"),
        name="new_kv_rows",
    )(*args)


def kernel(x_prompt, x_sample, cache_a_k, cache_a_v, cache_b_k, cache_b_v, cache_d_k, cache_d_v, state_c_conv, page_table, g_mix, w_in, lam_a, g_a_sub, b_c_glu, w_c_dw, b_c_dw, g_c_norm, b_c_norm, w_out, g_ffn, w_gate, w_up, w_down, g_final):
    n_b, s, d = x_prompt.shape
    n_seq, nq, _ = x_sample.shape
    depth = w_in.shape[0]
    assert n_b == 1, "prompt kernels keep one sequence's keys resident"

    def paged(c):
        assert c.shape[3:] == (N_HEADS, DH)
        return c.reshape(c.shape[0], c.shape[1], c.shape[2] * N_HEADS, DH)

    caches = [paged(c) for c in (cache_a_k, cache_a_v, cache_b_k, cache_b_v, cache_d_k, cache_d_v)]
    row = lambda v: v.reshape(1, -1)
    zero_buf = jnp.zeros((n_b, CONV_W - 1, GROUP_W), F32)

    xp = x_prompt.reshape(n_b * s, d)
    xs = x_sample.reshape(n_seq * nq, d)
    projs_p, projs_s, bufs_p, bufs_s = [], [], [], []

    for li in range(depth):
        w_in_l, w_out_l = w_in[li].astype(BF16), w_out[li].astype(BF16)
        wg_l, wu_l, wd_l = w_gate[li].astype(BF16), w_up[li].astype(BF16), w_down[li].astype(BF16)
        lam_init = 0.8 - 0.6 * math.exp(-0.3 * li)
        conv_w = (row(b_c_glu[li]), w_c_dw[li], row(b_c_dw[li]), row(g_c_norm[li]), row(b_c_norm[li]))
        last = li == depth - 1

        proj, proj_b = _norm_matmul(xp, row(g_mix[li]), w_in_l, with_bf16=True)
        o_a = _attn_a(proj, proj_b, s, lam_a[li], row(g_a_sub[li]), lam_init)
        o_b = _attn_b(proj, proj_b, s)
        o_c, buf_p = _conv_group(proj, n_b, s, zero_buf, *conv_w, out_dtype=BF16)
        o_d = _attn_d(proj, proj_b, s)
        xp = _out_proj(xp, (o_a, o_b, o_c, o_d), w_out_l)
        xp = _ffn(xp, row(g_ffn[li]), wg_l, wu_l, wd_l, row(g_final), final_norm=last)
        projs_p.append(proj)
        bufs_p.append(buf_p)

        (proj,) = _norm_matmul(xs, row(g_mix[li]), w_in_l, with_bf16=False)
        o_a = _dec_attention("a", proj, caches[0], caches[1], page_table, li, nq,
                             (COL_QA, COL_KA, COL_VA), extra=(lam_a[li], row(g_a_sub[li])),
                             lam_init=lam_init)
        o_b = _dec_attention("b", proj, caches[2], caches[3], page_table, li, nq,
                             (COL_QB, COL_KB, COL_VB))
        o_c, buf_s = _conv_group(proj, n_seq, nq, state_c_conv[li], *conv_w, out_dtype=F32)
        o_d = _dec_stickbreak(proj, caches[4], caches[5], page_table, li, nq)
        xs = _out_proj(xs, (o_a, o_b, o_c, o_d), w_out_l)
        xs = _ffn(xs, row(g_ffn[li]), wg_l, wu_l, wd_l, row(g_final), final_norm=last)
        projs_s.append(proj)
        bufs_s.append(buf_s)

    outs_p = [o.reshape(depth, n_b, s, N_HEADS, DH) for o in _new_rows(projs_p)]
    outs_s = [o.reshape(depth, n_seq, nq, N_HEADS, DH) for o in _new_rows(projs_s)]
    return (xp.reshape(n_b, s, d), xs.reshape(n_seq, nq, d),
            *outs_p, jnp.stack(bufs_p, axis=0), *outs_s, jnp.stack(bufs_s, axis=0))
```

```python
import functools
import math

import jax
import jax.numpy as jnp
from jax import lax
from jax.experimental import pallas as pl
from jax.experimental.pallas import tpu as pltpu

F32 = jnp.float32
BF16 = jnp.bfloat16

EPS = 1e-6
N_HEADS = 4
DH = 128
GROUP_W = N_HEADS * DH
DH_A = DH // 2
MOBA_BLOCK = 256
MOBA_TOPK = 3
CONV_W = 31
CONV_HALO = 32
N_NORM_GROUPS = 4
NEG_INF = float("-inf")
M_INIT = -1e30
MASK_BIAS = -2.0 ** 100
LOG2E = math.log2(math.e)
SB_DEAD = -150.0
V7X_LANES = 128
V7X_SUBLANES = 8
VMEM_LIMIT_BYTES = 56 * 1024 * 1024

COL_QA, COL_KA, COL_VA, COL_QB, COL_KB, COL_VB, COL_GLU_A, COL_GLU_G, COL_QD, COL_KD, COL_VD = range(11)

_NT = (((1,), (1,)), ((), ()))
_SMALL_ROW_TILES = (512, 256, 128, 64, 32, 16, 8)


def _dot(a, b):
    return jnp.dot(a, b, preferred_element_type=F32)


def _dot_nt(a, b):
    return lax.dot_general(a, b, _NT, preferred_element_type=F32)


def _rms(x, g):
    return x * lax.rsqrt(jnp.mean(x * x, axis=-1, keepdims=True) + EPS) * g


def _shift_div(x, n):
    assert n & (n - 1) == 0
    return lax.shift_right_logical(x, n.bit_length() - 1)


def _pick(n, candidates):
    for c in candidates:
        if n % c == 0:
            return c
    raise ValueError(f"no tile for {n} in {candidates}")


def _params(*sem):
    return pltpu.CompilerParams(dimension_semantics=sem, vmem_limit_bytes=VMEM_LIMIT_BYTES)


def _norm_matmul_kernel(x_ref, g_ref, w_ref, o_ref, *rest):
    h_ref = rest[-1]

    @pl.when(pl.program_id(1) == 0)
    def _():
        h_ref[...] = _rms(x_ref[...], g_ref[...]).astype(BF16)

    y = _dot(h_ref[...], w_ref[...])
    o_ref[...] = y
    if len(rest) == 2:
        rest[0][...] = y.astype(BF16)


def _norm_matmul(x, g, w, with_bf16):
    t, d = x.shape
    n = w.shape[1]
    tm = _pick(t, (1024,) + _SMALL_ROW_TILES)
    tn = _pick(n, (1408, 512, 256, 128))
    out_spec = pl.BlockSpec((tm, tn), lambda i, j: (i, j))
    out_shape = [jax.ShapeDtypeStruct((t, n), F32)]
    if with_bf16:
        out_shape.append(jax.ShapeDtypeStruct((t, n), BF16))
    return pl.pallas_call(
        _norm_matmul_kernel,
        out_shape=out_shape,
        grid=(t // tm, n // tn),
        in_specs=[pl.BlockSpec((tm, d), lambda i, j: (i, 0)),
                  pl.BlockSpec((1, d), lambda i, j: (0, 0)),
                  pl.BlockSpec((d, tn), lambda i, j: (0, j))],
        out_specs=[out_spec] * len(out_shape),
        scratch_shapes=[pltpu.VMEM((tm, d), BF16)],
        compiler_params=_params("parallel", "arbitrary"),
        name="norm_in_proj",
    )(x, g, w)


def _out_proj_kernel(x_ref, a_ref, b_ref, c_ref, d_ref, w_ref, o_ref):
    acc = x_ref[...]
    for gi, m_ref in enumerate((a_ref, b_ref, c_ref, d_ref)):
        acc = acc + _dot(m_ref[...].astype(BF16), w_ref[gi * GROUP_W:(gi + 1) * GROUP_W, :])
    o_ref[...] = acc


def _out_proj(x, mixes, w):
    t, d = x.shape
    tm = _pick(t, (1024,) + _SMALL_ROW_TILES)
    mix_spec = pl.BlockSpec((tm, GROUP_W), lambda i: (i, 0))
    return pl.pallas_call(
        _out_proj_kernel,
        out_shape=jax.ShapeDtypeStruct((t, d), F32),
        grid=(t // tm,),
        in_specs=[pl.BlockSpec((tm, d), lambda i: (i, 0)), mix_spec, mix_spec, mix_spec, mix_spec,
                  pl.BlockSpec(w.shape, lambda i: (0, 0))],
        out_specs=pl.BlockSpec((tm, d), lambda i: (i, 0)),
        compiler_params=_params("parallel"),
        name="out_proj",
    )(x, *mixes, w)


def _ffn_kernel(x_ref, g_ref, wg_ref, wu_ref, wd_ref, gf_ref, o_ref, h_ref, acc_ref, *, final_norm):
    j = pl.program_id(1)

    @pl.when(j == 0)
    def _():
        h_ref[...] = _rms(x_ref[...], g_ref[...]).astype(BF16)
        acc_ref[...] = jnp.zeros_like(acc_ref)

    h = h_ref[...]
    gate = _dot(h, wg_ref[...])
    up = _dot(h, wu_ref[...])
    act = (gate * jax.nn.sigmoid(gate) * up).astype(BF16)
    acc_ref[...] += _dot(act, wd_ref[...])

    @pl.when(j == pl.num_programs(1) - 1)
    def _():
        y = x_ref[...] + acc_ref[...]
        if final_norm:
            y = _rms(y, gf_ref[...])
        o_ref[...] = y


def _ffn(x, g, wg, wu, wd, g_final, final_norm):
    t, d = x.shape
    f = wg.shape[1]
    tm = _pick(t, _SMALL_ROW_TILES)
    tf = _pick(f, (512, 256, 128))
    return pl.pallas_call(
        functools.partial(_ffn_kernel, final_norm=final_norm),
        out_shape=jax.ShapeDtypeStruct((t, d), F32),
        grid=(t // tm, f // tf),
        in_specs=[pl.BlockSpec((tm, d), lambda i, j: (i, 0)),
                  pl.BlockSpec((1, d), lambda i, j: (0, 0)),
                  pl.BlockSpec((d, tf), lambda i, j: (0, j)),
                  pl.BlockSpec((d, tf), lambda i, j: (0, j)),
                  pl.BlockSpec((tf, d), lambda i, j: (j, 0)),
                  pl.BlockSpec((1, d), lambda i, j: (0, 0))],
        out_specs=pl.BlockSpec((tm, d), lambda i, j: (i, 0)),
        scratch_shapes=[pltpu.VMEM((tm, d), BF16), pltpu.VMEM((tm, d), F32)],
        compiler_params=_params("parallel", "arbitrary"),
        name="ffn",
    )(x, g, wg, wu, wd, g_final)


def _lambda(lam_ref, lam_init):
    lp = lam_ref[...]
    s01 = jnp.sum(lp[0:1, :] * lp[1:2, :], axis=1, keepdims=True)
    s23 = jnp.sum(lp[2:3, :] * lp[3:4, :], axis=1, keepdims=True)
    return jnp.exp(s01) - jnp.exp(s23) + lam_init


def _split_maps(q):
    lane = lax.broadcasted_iota(jnp.int32, q.shape, 1)
    return jnp.concatenate([jnp.where(lane < DH_A, q, 0.0), jnp.where(lane >= DH_A, q, 0.0)], axis=0)


def _softmax_update(s, m_ref, l_ref):
    m_prev = m_ref[...]
    m_new = jnp.maximum(m_prev, jnp.max(s, axis=1, keepdims=True))
    alpha = jnp.exp2(m_prev - m_new)
    p = jnp.exp2(s - m_new)
    l_ref[...] = alpha * l_ref[...] + jnp.sum(p, axis=1, keepdims=True)
    m_ref[...] = m_new
    return p, alpha


def _flash_loop(segments, n_total, scores_fn, values_fn, m_ref, acc_ref, s_ref, p_ref, a_ref):
    reps = s_ref.shape[1] // V7X_LANES
    m_ref[...] = jnp.full_like(m_ref, M_INIT)
    acc_ref[...] = jnp.zeros_like(acc_ref)
    p_ref[...] = jnp.zeros_like(p_ref)
    a_ref[...] = jnp.ones_like(a_ref)
    s_ref[...] = scores_fn(0)

    def finish(j):
        acc_ref[:, :DH] = a_ref[...] * acc_ref[:, :DH] + _dot(p_ref[...], values_fn(j))

    def make_body(keep_fn):
        def body(j, carry):
            finish(jnp.maximum(j - 1, 0))
            s = s_ref[...]
            s_ref[...] = scores_fn(jnp.minimum(j + 1, n_total - 1))
            if keep_fn is not None:
                s = jnp.where(keep_fn(j), s, NEG_INF)
            m_prev = m_ref[...]
            m_new = jnp.maximum(m_prev, jnp.max(s, axis=1, keepdims=True))
            alpha = jnp.exp2(m_prev - m_new)
            p = jnp.exp2(s - jnp.concatenate([m_new] * reps, axis=1))
            acc_ref[:, DH:] = alpha * acc_ref[:, DH:] + jnp.sum(p, axis=1, keepdims=True)
            a_ref[...] = alpha
            m_ref[...] = m_new
            p_ref[...] = p.astype(BF16)
            return carry
        return body

    for lo, hi, keep_fn in segments:
        lax.fori_loop(lo, hi, make_body(keep_fn), 0)
    finish(n_total - 1)
    acc = acc_ref[...]
    return acc[:, :DH], acc[:, DH:]


def _flash_scratch(rows, tk):
    return [pltpu.VMEM((rows, V7X_LANES), F32), pltpu.VMEM((rows, 2 * DH), F32),
            pltpu.VMEM((rows, tk), F32), pltpu.VMEM((rows, tk), BF16),
            pltpu.VMEM((rows, V7X_LANES), F32)]


def _softplus(z):
    return jnp.maximum(z, 0.0) + jnp.log(1.0 + jnp.exp(-jnp.abs(z)))


def _later_sum(lk, upper):
    hi = lk.astype(BF16)
    lo = (lk - hi.astype(F32)).astype(BF16)
    return _dot(hi, upper) + _dot(lo, upper)


def _suffix_matrix(n):
    j = lax.broadcasted_iota(jnp.int32, (n, n), 0)
    s = lax.broadcasted_iota(jnp.int32, (n, n), 1)
    return (j > s).astype(BF16)


def _attn_a_kernel(lam_ref, gsub_ref, q_ref, k_ref, v_ref, o_ref, *scratch, tq, tk, lam_init):
    qi = pl.program_id(1)
    qq = _split_maps(q_ref[...] * (DH_A ** -0.5 * LOG2E)).astype(BF16)

    def block(ref, j):
        return ref[pl.ds(pl.multiple_of(j * tk, tk), tk), :]

    def causal(j):
        row = lax.broadcasted_iota(jnp.int32, (2 * tq, tk), 0)
        col = lax.broadcasted_iota(jnp.int32, (2 * tq, tk), 1)
        q_pos = qi * tq + jnp.where(row >= tq, row - tq, row)
        return j * tk + col <= q_pos

    n_full = (qi * tq) // tk
    n_total = (qi * tq + tq - 1) // tk + 1
    acc, l = _flash_loop([(0, n_full, None), (n_full, n_total, causal)], n_total,
                         lambda j: _dot_nt(qq, block(k_ref, j)), lambda j: block(v_ref, j), *scratch)
    lam = _lambda(lam_ref, lam_init)
    o = acc[:tq] / l[:tq] - lam * (acc[tq:] / l[tq:])
    o_ref[...] = (_rms(o, gsub_ref[...]) * (1.0 - lam_init)).astype(o_ref.dtype)


def _attn_a(proj, proj_b, s, lam_a, g_sub, lam_init):
    tq = _pick(s, (256, 128))
    tk = _pick(s, (512, 256, 128))
    return pl.pallas_call(
        functools.partial(_attn_a_kernel, tq=tq, tk=tk, lam_init=lam_init),
        out_shape=jax.ShapeDtypeStruct((s, GROUP_W), BF16),
        grid=(N_HEADS, s // tq),
        in_specs=[pl.BlockSpec(lam_a.shape, lambda h, i: (0, 0)),
                  pl.BlockSpec((1, DH), lambda h, i: (0, 0)),
                  pl.BlockSpec((tq, DH), lambda h, i: (i, COL_QA * N_HEADS + h)),
                  pl.BlockSpec((s, DH), lambda h, i: (0, COL_KA * N_HEADS + h)),
                  pl.BlockSpec((s, DH), lambda h, i: (0, COL_VA * N_HEADS + h))],
        out_specs=pl.BlockSpec((tq, DH), lambda h, i: (i, h)),
        scratch_shapes=_flash_scratch(2 * tq, tk),
        compiler_params=_params("parallel", "arbitrary"),
        name="prompt_diff_attn",
    )(lam_a, g_sub, proj, proj_b, proj_b)


def _block_mean_kernel(k_ref, o_ref, *, n_blocks):
    o_ref[...] = jnp.zeros_like(o_ref)

    def body(b, carry):
        rows = k_ref[pl.ds(pl.multiple_of(b * MOBA_BLOCK, MOBA_BLOCK), MOBA_BLOCK), :]
        o_ref[pl.ds(b, 1), :] = jnp.mean(rows, axis=0, keepdims=True)
        return carry

    lax.fori_loop(0, n_blocks, body, 0)


def _block_means(proj, s, n_pad):
    return pl.pallas_call(
        functools.partial(_block_mean_kernel, n_blocks=s // MOBA_BLOCK),
        out_shape=jax.ShapeDtypeStruct((N_HEADS, n_pad, DH), F32),
        grid=(N_HEADS,),
        in_specs=[pl.BlockSpec((s, DH), lambda h: (0, COL_KB * N_HEADS + h))],
        out_specs=pl.BlockSpec((None, n_pad, DH), lambda h: (h, 0, 0)),
        compiler_params=_params("parallel"),
        name="moba_block_means",
    )(proj)


def _moba_select(gates, own):
    lane = lax.broadcasted_iota(jnp.int32, gates.shape, 1)
    g = jnp.where(lane < own, gates, NEG_INF)
    sel = jnp.where(lane == own, 1.0, 0.0)
    for _ in range(MOBA_TOPK):
        mx = jnp.max(g, axis=1, keepdims=True)
        pick = jnp.min(jnp.where(g == mx, lane, gates.shape[1]), axis=1, keepdims=True)
        hit = jnp.logical_and(lane == pick, mx > NEG_INF)
        sel = jnp.where(hit, 1.0, sel)
        g = jnp.where(lane == pick, NEG_INF, g)
    return sel


def _attn_b_kernel(q_ref, kmean_ref, k_ref, v_ref, o_ref, *scratch, t):
    qi = pl.program_id(1)
    per_tile = t // MOBA_BLOCK
    q_raw = q_ref[...]
    gates = lax.dot_general(q_raw, kmean_ref[...], _NT, precision=lax.Precision.HIGHEST,
                            preferred_element_type=F32)
    row1 = lax.broadcasted_iota(jnp.int32, (t, 1), 0)
    own = qi * per_tile + _shift_div(row1, MOBA_BLOCK)
    sel = _moba_select(gates, own)
    q = jnp.concatenate([(q_raw * (DH ** -0.5 * LOG2E)).astype(BF16),
                         jnp.where(sel > 0.5, 0.0, MASK_BIAS).astype(BF16)], axis=1)

    def block(ref, j):
        return ref[pl.ds(pl.multiple_of(j * t, t), t), :]

    def scores(j):
        key = lax.broadcasted_iota(jnp.int32, (t, sel.shape[1]), 0)
        blk = lax.broadcasted_iota(jnp.int32, (t, sel.shape[1]), 1)
        onehot = (blk == j * per_tile + _shift_div(key, MOBA_BLOCK)).astype(BF16)
        return _dot_nt(q, jnp.concatenate([block(k_ref, j), onehot], axis=1))

    def causal(j):
        row = lax.broadcasted_iota(jnp.int32, (t, t), 0)
        col = lax.broadcasted_iota(jnp.int32, (t, t), 1)
        return col <= row

    acc, l = _flash_loop([(0, qi, None), (qi, qi + 1, causal)], qi + 1,
                         scores, lambda j: block(v_ref, j), *scratch)
    o_ref[...] = (acc / l).astype(o_ref.dtype)


def _attn_b(proj, proj_b, s):
    t = _pick(s, (512, 256))
    n_pad = -(-(s // MOBA_BLOCK) // V7X_LANES) * V7X_LANES
    kmean = _block_means(proj, s, n_pad)
    return pl.pallas_call(
        functools.partial(_attn_b_kernel, t=t),
        out_shape=jax.ShapeDtypeStruct((s, GROUP_W), BF16),
        grid=(N_HEADS, s // t),
        in_specs=[pl.BlockSpec((t, DH), lambda h, i: (i, COL_QB * N_HEADS + h)),
                  pl.BlockSpec((None, n_pad, DH), lambda h, i: (h, 0, 0)),
                  pl.BlockSpec((s, DH), lambda h, i: (0, COL_KB * N_HEADS + h)),
                  pl.BlockSpec((s, DH), lambda h, i: (0, COL_VB * N_HEADS + h))],
        out_specs=pl.BlockSpec((t, DH), lambda h, i: (i, h)),
        scratch_shapes=_flash_scratch(t, t),
        compiler_params=_params("parallel", "arbitrary"),
        name="prompt_moba",
    )(proj, kmean, proj_b, proj_b)


def _attn_d_kernel(q_ref, k_ref, v_ref, o_ref, r_ref, acc_ref, *, tq, tk):
    qi = pl.program_id(1)
    q = (q_ref[...] * (DH ** -0.5)).astype(BF16)
    r_ref[...] = jnp.zeros_like(r_ref)
    acc_ref[...] = jnp.zeros_like(acc_ref)

    def cond(carry):
        kb, r_max = carry
        return jnp.logical_and(kb >= 0, r_max > SB_DEAD)

    def body(carry):
        kb, _ = carry
        start = pl.multiple_of(kb * tk, tk)
        z = _dot_nt(q, k_ref[pl.ds(start, tk), :])
        row = lax.broadcasted_iota(jnp.int32, (tq, tk), 0)
        col = lax.broadcasted_iota(jnp.int32, (tq, tk), 1)
        strict = start + col < qi * tq + row
        sp = _softplus(z)
        lk = jnp.where(strict, -sp, 0.0)
        later = _later_sum(lk, _suffix_matrix(tk))
        r = r_ref[...]
        w = jnp.where(strict, jnp.exp(z - sp + later + r), 0.0)
        acc_ref[...] += _dot(w.astype(BF16), v_ref[pl.ds(start, tk), :])
        r_new = r + jnp.sum(lk, axis=1, keepdims=True)
        r_ref[...] = r_new
        return kb - 1, jnp.max(r_new)

    lax.while_loop(cond, body, ((qi * tq + tq - 1) // tk, jnp.float32(0.0)))
    o_ref[...] = acc_ref[...].astype(o_ref.dtype)


def _attn_d(proj, proj_b, s):
    tq = _pick(s, (256, 128))
    tk = _pick(s, (256, 128))
    return pl.pallas_call(
        functools.partial(_attn_d_kernel, tq=tq, tk=tk),
        out_shape=jax.ShapeDtypeStruct((s, GROUP_W), BF16),
        grid=(N_HEADS, s // tq),
        in_specs=[pl.BlockSpec((tq, DH), lambda h, i: (i, COL_QD * N_HEADS + h)),
                  pl.BlockSpec((s, DH), lambda h, i: (0, COL_KD * N_HEADS + h)),
                  pl.BlockSpec((s, DH), lambda h, i: (0, COL_VD * N_HEADS + h))],
        out_specs=pl.BlockSpec((tq, DH), lambda h, i: (i, h)),
        scratch_shapes=[pltpu.VMEM((tq, 1), F32), pltpu.VMEM((tq, DH), F32)],
        compiler_params=_params("parallel", "arbitrary"),
        name="prompt_stickbreak",
    )(proj, proj_b, proj_b)


def _conv_kernel(*refs, tt, has_prev):
    if has_prev:
        cur_ref, prev_ref, cbuf_ref, bglu_ref, wdw_ref, bdw_ref, gn_ref, bn_ref, o_ref, nbuf_ref, ubuf_ref = refs
    else:
        cur_ref, cbuf_ref, bglu_ref, wdw_ref, bdw_ref, gn_ref, bn_ref, o_ref, nbuf_ref, ubuf_ref = refs
    i = pl.program_id(1)
    c = GROUP_W

    def glu(x):
        x = x + bglu_ref[...]
        return x[:, :c] * jax.nn.sigmoid(x[:, c:])

    ubuf_ref[CONV_HALO:CONV_HALO + tt, :] = glu(cur_ref[...])

    @pl.when(i == 0)
    def _():
        ubuf_ref[CONV_HALO - (CONV_W - 1):CONV_HALO, :] = cbuf_ref[...]

    if has_prev:
        @pl.when(i > 0)
        def _():
            ubuf_ref[0:CONV_HALO, :] = glu(prev_ref[...])

    off = CONV_HALO - (CONV_W - 1)
    y = jnp.zeros((tt, c), F32) + bdw_ref[...]
    for j in range(CONV_W):
        y = y + ubuf_ref[off + j:off + j + tt, :] * wdw_ref[j:j + 1, :]

    gw = c // N_NORM_GROUPS
    for gi in range(N_NORM_GROUPS):
        seg = y[:, gi * gw:(gi + 1) * gw]
        mu = jnp.mean(seg, axis=-1, keepdims=True)
        var = jnp.mean(jnp.square(seg - mu), axis=-1, keepdims=True)
        yn = (seg - mu) * lax.rsqrt(var + EPS)
        yn = yn * gn_ref[:, gi * gw:(gi + 1) * gw] + bn_ref[:, gi * gw:(gi + 1) * gw]
        o_ref[:, gi * gw:(gi + 1) * gw] = (yn * jax.nn.sigmoid(yn)).astype(o_ref.dtype)

    @pl.when(i == pl.num_programs(1) - 1)
    def _():
        nbuf_ref[...] = ubuf_ref[CONV_HALO + tt - (CONV_W - 1):CONV_HALO + tt, :]


def _conv_group(proj, batch, s, conv_buf, b_glu, w_dw, b_dw, g_norm, b_norm, out_dtype):
    tt = _pick(s, _SMALL_ROW_TILES)
    nt = s // tt
    has_prev = nt > 1
    assert tt + CONV_HALO >= CONV_W - 1 and (not has_prev or tt % CONV_HALO == 0)
    c = GROUP_W
    glu_col = COL_GLU_A // 2
    in_specs = [pl.BlockSpec((tt, 2 * c), lambda b, i: (b * nt + i, glu_col))]
    args = [proj]
    if has_prev:
        per = tt // CONV_HALO
        in_specs.append(pl.BlockSpec((CONV_HALO, 2 * c),
                                     lambda b, i: (jnp.maximum((b * nt + i) * per - 1, 0), glu_col)))
        args.append(proj)
    vec = lambda n: pl.BlockSpec((1, n), lambda b, i: (0, 0))
    in_specs += [pl.BlockSpec((None, CONV_W - 1, c), lambda b, i: (b, 0, 0)),
                 vec(2 * c), pl.BlockSpec((CONV_W, c), lambda b, i: (0, 0)), vec(c), vec(c), vec(c)]
    args += [conv_buf, b_glu, w_dw, b_dw, g_norm, b_norm]
    return pl.pallas_call(
        functools.partial(_conv_kernel, tt=tt, has_prev=has_prev),
        out_shape=(jax.ShapeDtypeStruct((batch * s, c), out_dtype),
                   jax.ShapeDtypeStruct((batch, CONV_W - 1, c), F32)),
        grid=(batch, nt),
        in_specs=in_specs,
        out_specs=(pl.BlockSpec((tt, c), lambda b, i: (b * nt + i, 0)),
                   pl.BlockSpec((None, CONV_W - 1, c), lambda b, i: (b, 0, 0))),
        scratch_shapes=[pltpu.VMEM((CONV_HALO + tt, c), F32)],
        compiler_params=_params("parallel", "arbitrary"),
        name="conv_module",
    )(*args)


def _page_heads(ref, page, dtype):
    return [ref[pl.ds(h, page, stride=N_HEADS), :].astype(dtype) for h in range(N_HEADS)]


def _pad_heads(pad_ref, new_ref, dtype):
    pad_ref[...] = jnp.zeros_like(pad_ref)
    pad_ref[0:new_ref.shape[0], :] = new_ref[...]
    return [pad_ref[:, h * DH:(h + 1) * DH].astype(dtype) for h in range(N_HEADS)]


def _head_rows(fn):
    return jnp.concatenate([fn(h) for h in range(N_HEADS)], axis=0)


def _new_key_index(shape, nq):
    row = lax.broadcasted_iota(jnp.int32, shape, 0)
    col = lax.broadcasted_iota(jnp.int32, shape, 1)
    return col, row & (nq - 1)


def _dec_a_kernel(pt_ref, lam_ref, gsub_ref, q_ref, kn_ref, vn_ref, *rest, n_pg, nq, page, lam_init):
    del pt_ref
    k_pages, v_pages = rest[:n_pg], rest[n_pg:2 * n_pg]
    o_ref, qh_ref, kpad_ref, vpad_ref, m_ref, l_ref, acc_ref = rest[2 * n_pg:]
    g = pl.program_id(1)
    rh = 2 * nq

    def reduce(s, pv_fn):
        p, alpha = _softmax_update(s, m_ref, l_ref)
        pv = _head_rows(lambda h: pv_fn(h, p[h * rh:(h + 1) * rh].astype(BF16)))
        acc_ref[...] = alpha * acc_ref[...] + pv

    @pl.when(g == 0)
    def _():
        for h in range(N_HEADS):
            qh = q_ref[:, h * DH:(h + 1) * DH] * (DH_A ** -0.5 * LOG2E)
            qh_ref[h] = _split_maps(qh).astype(BF16)
        m_ref[...] = jnp.full_like(m_ref, M_INIT)
        l_ref[...] = jnp.zeros_like(l_ref)
        acc_ref[...] = jnp.zeros_like(acc_ref)
        kh = _pad_heads(kpad_ref, kn_ref, BF16)
        vh = _pad_heads(vpad_ref, vn_ref, BF16)
        s = _head_rows(lambda h: _dot_nt(qh_ref[h], kh[h]))
        key, qry = _new_key_index(s.shape, nq)
        reduce(jnp.where(key <= qry, s, NEG_INF), lambda h, ph: _dot(ph, vh[h]))

    kh = [_page_heads(r, page, BF16) for r in k_pages]
    vh = [_page_heads(r, page, BF16) for r in v_pages]
    keys_of = lambda pages, h: jnp.concatenate([pages[t][h] for t in range(n_pg)], axis=0)
    s = _head_rows(lambda h: _dot_nt(qh_ref[h], keys_of(kh, h)))
    reduce(s, lambda h, ph: _dot(ph, keys_of(vh, h)))

    @pl.when(g == pl.num_programs(1) - 1)
    def _():
        acc = acc_ref[...]
        l = l_ref[...]
        lam = _lambda(lam_ref, lam_init)
        for h in range(N_HEADS):
            r1, r2 = h * rh, h * rh + nq
            o = acc[r1:r1 + nq] / l[r1:r1 + nq] - lam * (acc[r2:r2 + nq] / l[r2:r2 + nq])
            o_ref[:, h * DH:(h + 1) * DH] = _rms(o, gsub_ref[...]) * (1.0 - lam_init)


def _stickbreak_pages(qh_ref, k_pages, v_pages, r_ref, acc_ref, rmax_ref, nq, page):
    def step(kh, vh, strict, r):
        z = _head_rows(lambda h: _dot_nt(qh_ref[h], kh[h]))
        sp = _softplus(z)
        lk = -sp if strict is None else jnp.where(strict(z.shape), -sp, 0.0)
        later = _later_sum(lk, _suffix_matrix(page))
        w = jnp.exp(z - sp + later + r)
        if strict is not None:
            w = jnp.where(strict(z.shape), w, 0.0)
        acc_ref[...] += _head_rows(lambda h: _dot(w[h * nq:(h + 1) * nq].astype(BF16), vh[h]))
        r_new = r + jnp.sum(lk, axis=1, keepdims=True)
        r_ref[...] = r_new
        rmax_ref[0] = jnp.max(r_new)

    def run_pages():
        for k_ref, v_ref in zip(k_pages, v_pages):
            @pl.when(rmax_ref[0] > SB_DEAD)
            def _():
                step(_page_heads(k_ref, page, BF16), _page_heads(v_ref, page, BF16), None, r_ref[...])

    return step, run_pages


def _dec_d_head_kernel(pt_ref, q_ref, kn_ref, vn_ref, *rest, n_pg, nq, page):
    del pt_ref
    k_pages, v_pages = rest[:n_pg], rest[n_pg:2 * n_pg]
    acc_out, r_out, qh_ref, kpad_ref, vpad_ref, r_ref, acc_ref, rmax_ref = rest[2 * n_pg:]
    step, run_pages = _stickbreak_pages(qh_ref, k_pages, v_pages, r_ref, acc_ref, rmax_ref, nq, page)
    for h in range(N_HEADS):
        qh_ref[h] = (q_ref[:, h * DH:(h + 1) * DH] * (DH ** -0.5)).astype(BF16)
    acc_ref[...] = jnp.zeros_like(acc_ref)

    def strict(shape):
        key, qry = _new_key_index(shape, nq)
        return key < qry

    step(_pad_heads(kpad_ref, kn_ref, BF16), _pad_heads(vpad_ref, vn_ref, BF16), strict,
         jnp.zeros(r_ref.shape, F32))
    run_pages()
    acc_out[...] = acc_ref[...]
    r_out[...] = jnp.broadcast_to(r_ref[...], r_out.shape)


def _dec_d_tail_kernel(pt_ref, dead_ref, q_ref, acc_in, r_in, *rest, n_pg, nq, page):
    del pt_ref
    k_pages, v_pages = rest[:n_pg], rest[n_pg:2 * n_pg]
    o_ref, qh_ref, r_ref, acc_ref, rmax_ref = rest[2 * n_pg:]
    g = pl.program_id(1)
    _, run_pages = _stickbreak_pages(qh_ref, k_pages, v_pages, r_ref, acc_ref, rmax_ref, nq, page)

    @pl.when(g == 0)
    def _():
        for h in range(N_HEADS):
            qh_ref[h] = (q_ref[:, h * DH:(h + 1) * DH] * (DH ** -0.5)).astype(BF16)
        acc_ref[...] = acc_in[...]
        r = r_in[:, 0:1]
        r_ref[...] = r
        rmax_ref[0] = jnp.where(dead_ref[pl.program_id(0)] == 1, NEG_INF, jnp.max(r))

    run_pages()

    @pl.when(g == pl.num_programs(1) - 1)
    def _():
        acc = acc_ref[...]
        for h in range(N_HEADS):
            o_ref[:, h * DH:(h + 1) * DH] = acc[h * nq:(h + 1) * nq]


def _dec_b_kernel(pt_ref, q_ref, kn_ref, vn_ref, *rest, n_pg, nq, page, n_blocks):
    del pt_ref
    k_pages, v_pages = rest[:n_pg], rest[n_pg:2 * n_pg]
    o_ref, qf_ref, qh_ref, kpad_ref, vpad_ref, g_ref, m_ref, l_ref, acc_ref = rest[2 * n_pg:]
    g = pl.program_id(1)
    per_block = MOBA_BLOCK // page
    lane = lax.broadcasted_iota(jnp.int32, g_ref.shape, 1)

    @pl.when(g == 0)
    def _():
        for h in range(N_HEADS):
            qh = q_ref[:, h * DH:(h + 1) * DH]
            qf_ref[h] = qh
            qh_ref[h] = (qh * (DH ** -0.5 * LOG2E)).astype(BF16)
        g_ref[...] = jnp.full_like(g_ref, NEG_INF)
        m_ref[...] = jnp.full_like(m_ref, M_INIT)
        l_ref[...] = jnp.zeros_like(l_ref)

    gates, maxes, sums = g_ref[...], m_ref[...], l_ref[...]
    kf = [_page_heads(r, page, F32) for r in k_pages]
    vh = [_page_heads(r, page, BF16) for r in v_pages]
    s_all = _head_rows(lambda h: _dot_nt(
        qh_ref[h], jnp.concatenate([kf[t][h].astype(BF16) for t in range(n_pg)], axis=0)))
    for blk in range(n_pg // per_block):
        pages = range(blk * per_block, (blk + 1) * per_block)

        def gate_of(h):
            k_mean = sum(jnp.sum(kf[t][h], axis=0, keepdims=True) for t in pages)
            return jnp.sum(qf_ref[h] * (k_mean * (1.0 / MOBA_BLOCK)), axis=1, keepdims=True)

        gate = _head_rows(gate_of)
        s = s_all[:, blk * MOBA_BLOCK:(blk + 1) * MOBA_BLOCK]
        m_b = jnp.max(s, axis=1, keepdims=True)
        p = jnp.exp2(s - m_b)
        l_b = jnp.sum(p, axis=1, keepdims=True)
        acc_b = _head_rows(lambda h: _dot(p[h * nq:(h + 1) * nq].astype(BF16),
                                          jnp.concatenate([vh[t][h] for t in pages], axis=0)))
        b_idx = g * (n_pg // per_block) + blk
        acc_ref[b_idx] = acc_b
        here = lane == b_idx
        gates = jnp.where(here, gate, gates)
        maxes = jnp.where(here, m_b, maxes)
        sums = jnp.where(here, l_b, sums)
    g_ref[...], m_ref[...], l_ref[...] = gates, maxes, sums

    @pl.when(g == pl.num_programs(1) - 1)
    def _():
        kh = _pad_heads(kpad_ref, kn_ref, BF16)
        vh = _pad_heads(vpad_ref, vn_ref, BF16)
        s = _head_rows(lambda h: _dot_nt(qh_ref[h], kh[h]))
        key, qry = _new_key_index(s.shape, nq)
        s = jnp.where(key <= qry, s, NEG_INF)
        m_o = jnp.max(s, axis=1, keepdims=True)
        p = jnp.exp2(s - m_o)
        chosen = jnp.logical_and(_moba_select(gates, n_blocks) > 0.5, lane < n_blocks)
        m_all = jnp.maximum(m_o, jnp.max(jnp.where(chosen, maxes, NEG_INF), axis=1, keepdims=True))
        weight = jnp.where(chosen, jnp.exp2(maxes - m_all), 0.0)
        w_o = jnp.exp2(m_o - m_all)
        den = w_o * jnp.sum(p, axis=1, keepdims=True) + jnp.sum(weight * sums, axis=1, keepdims=True)

        num = w_o * _head_rows(lambda h: _dot(p[h * nq:(h + 1) * nq].astype(BF16), vh[h]))
        for b in range(n_blocks):
            w_b = jnp.sum(jnp.where(lane == b, weight, 0.0), axis=1, keepdims=True)
            num = num + w_b * acc_ref[b]
        out = num / den
        for h in range(N_HEADS):
            o_ref[:, h * DH:(h + 1) * DH] = out[h * nq:(h + 1) * nq]


def _dec_attention(kind, proj, k_cache, v_cache, page_table, li, nq, cols, extra=(), **kw):
    n_seq, n_pages = page_table.shape
    page = k_cache.shape[2] // N_HEADS
    n_pg = _pick(n_pages, (32, 16, 8, 4, 2))
    steps = n_pages // n_pg
    col_q, col_k, col_v = cols
    w = GROUP_W
    assert nq % V7X_SUBLANES == 0 and nq & (nq - 1) == 0 and nq <= page
    assert (n_pages * page) % MOBA_BLOCK == 0 and MOBA_BLOCK % page == 0

    def page_spec(t):
        return pl.BlockSpec((None, None, page * N_HEADS, DH),
                            lambda b, g, pt: (li, pt[b, g * n_pg + t], 0, 0))

    small = lambda a: pl.BlockSpec(a.shape, lambda b, g, pt: (0,) * a.ndim)
    row_spec = lambda col: pl.BlockSpec((nq, w), lambda b, g, pt: (b, col))
    in_specs = ([small(a) for a in extra] + [row_spec(col_q), row_spec(col_k), row_spec(col_v)]
                + [page_spec(t) for t in range(n_pg)] * 2)
    args = list(extra) + [proj, proj, proj] + [k_cache] * n_pg + [v_cache] * n_pg

    pad = [pltpu.VMEM((page, w), F32), pltpu.VMEM((page, w), F32)]
    stat = lambda rows: pltpu.VMEM((rows, 1), F32)
    if kind == "a":
        rows = 2 * N_HEADS * nq
        body = functools.partial(_dec_a_kernel, n_pg=n_pg, nq=nq, page=page, **kw)
        scratch = [pltpu.VMEM((N_HEADS, 2 * nq, DH), BF16)] + pad + [
            stat(rows), stat(rows), pltpu.VMEM((rows, DH), F32)]
    else:
        rows = N_HEADS * nq
        n_blocks = n_pages * page // MOBA_BLOCK
        assert n_blocks <= V7X_LANES
        body = functools.partial(_dec_b_kernel, n_pg=n_pg, nq=nq, page=page, n_blocks=n_blocks)
        lanes = lambda: pltpu.VMEM((rows, V7X_LANES), F32)
        scratch = [pltpu.VMEM((N_HEADS, nq, DH), F32), pltpu.VMEM((N_HEADS, nq, DH), BF16)] + pad + [
            lanes(), lanes(), lanes(), pltpu.VMEM((n_blocks, rows, DH), F32)]

    return pl.pallas_call(
        body,
        out_shape=jax.ShapeDtypeStruct((n_seq * nq, w), F32),
        grid_spec=pltpu.PrefetchScalarGridSpec(
            num_scalar_prefetch=1,
            grid=(n_seq, steps),
            in_specs=in_specs,
            out_specs=pl.BlockSpec((nq, w), lambda b, g, pt: (b, 0)),
            scratch_shapes=scratch),
        compiler_params=_params("parallel", "arbitrary"),
        name="sample_attn_" + kind,
    )(page_table, *args)


def _dec_stickbreak(proj, k_cache, v_cache, page_table, li, nq):
    n_seq, n_pages = page_table.shape
    page = k_cache.shape[2] // N_HEADS
    n_head = 8 if n_pages >= 16 else n_pages // 2
    n_pg = _pick(n_pages - n_head, (8, 4, 2, 1))
    steps = (n_pages - n_head) // n_pg
    rows, w = N_HEADS * nq, GROUP_W
    assert 0 < n_head < n_pages
    page_block = (None, None, page * N_HEADS, DH)
    qh = pltpu.VMEM((N_HEADS, nq, DH), BF16)
    state = [pltpu.VMEM((rows, 1), F32), pltpu.VMEM((rows, DH), F32), pltpu.SMEM((1,), F32)]

    def head_page(t):
        return pl.BlockSpec(page_block, lambda b, pt: (li, pt[b, n_pages - 1 - t], 0, 0))

    row_spec = lambda col: pl.BlockSpec((nq, w), lambda b, pt: (b, col))
    acc0, r0 = pl.pallas_call(
        functools.partial(_dec_d_head_kernel, n_pg=n_head, nq=nq, page=page),
        out_shape=[jax.ShapeDtypeStruct((n_seq * rows, DH), F32),
                   jax.ShapeDtypeStruct((n_seq * rows, V7X_LANES), F32)],
        grid_spec=pltpu.PrefetchScalarGridSpec(
            num_scalar_prefetch=1,
            grid=(n_seq,),
            in_specs=[row_spec(COL_QD), row_spec(COL_KD), row_spec(COL_VD)]
            + [head_page(t) for t in range(n_head)] * 2,
            out_specs=[pl.BlockSpec((rows, DH), lambda b, pt: (b, 0)),
                       pl.BlockSpec((rows, V7X_LANES), lambda b, pt: (b, 0))],
            scratch_shapes=[qh, pltpu.VMEM((page, w), F32), pltpu.VMEM((page, w), F32)] + state),
        compiler_params=_params("parallel"),
        name="sample_stickbreak_head",
    )(page_table, proj, proj, proj, *([k_cache] * n_head), *([v_cache] * n_head))

    dead = (jnp.max(r0.reshape(n_seq, -1), axis=1) < SB_DEAD).astype(jnp.int32)

    def tail_page(t):
        def imap(b, g, pt, dd):
            idx = n_pages - 1 - n_head - (g * n_pg + t)
            return (li, jnp.where(dd[b] == 1, 0, pt[b, idx]), 0, 0)
        return pl.BlockSpec(page_block, imap)

    tail = pl.pallas_call(
        functools.partial(_dec_d_tail_kernel, n_pg=n_pg, nq=nq, page=page),
        out_shape=jax.ShapeDtypeStruct((n_seq * nq, w), F32),
        grid_spec=pltpu.PrefetchScalarGridSpec(
            num_scalar_prefetch=2,
            grid=(n_seq, steps),
            in_specs=[pl.BlockSpec((nq, w), lambda b, g, pt, dd: (b, COL_QD)),
                      pl.BlockSpec((rows, DH), lambda b, g, pt, dd: (b, 0)),
                      pl.BlockSpec((rows, V7X_LANES), lambda b, g, pt, dd: (b, 0))]
            + [tail_page(t) for t in range(n_pg)] * 2,
            out_specs=pl.BlockSpec((nq, w), lambda b, g, pt, dd: (b, 0)),
            scratch_shapes=[qh] + state),
        compiler_params=_params("parallel", "arbitrary"),
        name="sample_stickbreak_tail",
    )

    def run_tail():
        return tail(page_table, dead, proj, acc0, r0, *([k_cache] * n_pg), *([v_cache] * n_pg))

    def all_dead():
        return acc0.reshape(n_seq, N_HEADS, nq, DH).transpose(0, 2, 1, 3).reshape(n_seq * nq, w)

    return lax.cond(jnp.any(dead == 0), run_tail, all_dead)


KV_COLS = (COL_KA, COL_VA, COL_KB, COL_VB, COL_KD, COL_VD)


def _new_rows_kernel(*refs, n_layers, tm):
    n_in = n_layers * len(KV_COLS)
    ins, outs = refs[:n_in], refs[n_in:]
    for li in range(n_layers):
        @pl.when(pl.program_id(0) == li)
        def _():
            for gi, o_ref in enumerate(outs):
                src = ins[li * len(KV_COLS) + gi]
                for h in range(N_HEADS):
                    o_ref[pl.ds(h, tm, stride=N_HEADS), :] = src[:, h * DH:(h + 1) * DH]


def _new_rows(projs):
    n_layers = len(projs)
    t = projs[0].shape[0]
    tm = _pick(t, _SMALL_ROW_TILES)
    nt = t // tm

    def in_spec(li, col):
        def imap(d, i):
            return (jnp.where(d == li, i, jnp.where(d > li, nt - 1, 0)), col)
        return pl.BlockSpec((tm, GROUP_W), imap)

    in_specs = [in_spec(li, col) for li in range(n_layers) for col in KV_COLS]
    args = [projs[li] for li in range(n_layers) for _ in KV_COLS]
    out_shape = [jax.ShapeDtypeStruct((n_layers, t * N_HEADS, DH), F32)] * len(KV_COLS)
    out_specs = [pl.BlockSpec((None, tm * N_HEADS, DH), lambda d, i: (d, i, 0))] * len(KV_COLS)
    return pl.pallas_call(
        functools.partial(_new_rows_kernel, n_layers=n_layers, tm=tm),
        out_shape=out_shape,
        grid=(n_layers, nt),
        in_specs=in_specs,
        out_specs=out_specs,
        compiler_params=_params("arbitrary", "arbitrary"),
        name="new_kv_rows",
    )(*args)


def kernel(x_prompt, x_sample, cache_a_k, cache_a_v, cache_b_k, cache_b_v, cache_d_k, cache_d_v, state_c_conv, page_table, g_mix, w_in, lam_a, g_a_sub, b_c_glu, w_c_dw, b_c_dw, g_c_norm, b_c_norm, w_out, g_ffn, w_gate, w_up, w_down, g_final):
    n_b, s, d = x_prompt.shape
    n_seq, nq, _ = x_sample.shape
    depth = w_in.shape[0]
    assert n_b == 1, "prompt kernels keep one sequence's keys resident"

    def paged(c):
        assert c.shape[3:] == (N_HEADS, DH)
        return c.reshape(c.shape[0], c.shape[1], c.shape[2] * N_HEADS, DH)

    caches = [paged(c) for c in (cache_a_k, cache_a_v, cache_b_k, cache_b_v, cache_d_k, cache_d_v)]
    row = lambda v: v.reshape(1, -1)
    zero_buf = jnp.zeros((n_b, CONV_W - 1, GROUP_W), F32)

    xp = x_prompt.reshape(n_b * s, d)
    xs = x_sample.reshape(n_seq * nq, d)
    projs_p, projs_s, bufs_p, bufs_s = [], [], [], []

    for li in range(depth):
        w_in_l, w_out_l = w_in[li].astype(BF16), w_out[li].astype(BF16)
        wg_l, wu_l, wd_l = w_gate[li].astype(BF16), w_up[li].astype(BF16), w_down[li].astype(BF16)
        lam_init = 0.8 - 0.6 * math.exp(-0.3 * li)
        conv_w = (row(b_c_glu[li]), w_c_dw[li], row(b_c_dw[li]), row(g_c_norm[li]), row(b_c_norm[li]))
        last = li == depth - 1

        proj, proj_b = _norm_matmul(xp, row(g_mix[li]), w_in_l, with_bf16=True)
        o_a = _attn_a(proj, proj_b, s, lam_a[li], row(g_a_sub[li]), lam_init)
        o_b = _attn_b(proj, proj_b, s)
        o_c, buf_p = _conv_group(proj, n_b, s, zero_buf, *conv_w, out_dtype=BF16)
        o_d = _attn_d(proj, proj_b, s)
        xp = _out_proj(xp, (o_a, o_b, o_c, o_d), w_out_l)
        xp = _ffn(xp, row(g_ffn[li]), wg_l, wu_l, wd_l, row(g_final), final_norm=last)
        projs_p.append(proj)
        bufs_p.append(buf_p)

        (proj,) = _norm_matmul(xs, row(g_mix[li]), w_in_l, with_bf16=False)
        o_a = _dec_attention("a", proj, caches[0], caches[1], page_table, li, nq,
                             (COL_QA, COL_KA, COL_VA), extra=(lam_a[li], row(g_a_sub[li])),
                             lam_init=lam_init)
        o_b = _dec_attention("b", proj, caches[2], caches[3], page_table, li, nq,
                             (COL_QB, COL_KB, COL_VB))
        o_c, buf_s = _conv_group(proj, n_seq, nq, state_c_conv[li], *conv_w, out_dtype=F32)
        o_d = _dec_stickbreak(proj, caches[4], caches[5], page_table, li, nq)
        xs = _out_proj(xs, (o_a, o_b, o_c, o_d), w_out_l)
        xs = _ffn(xs, row(g_ffn[li]), wg_l, wu_l, wd_l, row(g_final), final_norm=last)
        projs_s.append(proj)
        bufs_s.append(buf_s)

    outs_p = [o.reshape(depth, n_b, s, N_HEADS, DH) for o in _new_rows(projs_p)]
    outs_s = [o.reshape(depth, n_seq, nq, N_HEADS, DH) for o in _new_rows(projs_s)]
    return (xp.reshape(n_b, s, d), xs.reshape(n_seq, nq, d),
            *outs_p, jnp.stack(bufs_p, axis=0), *outs_s, jnp.stack(bufs_s, axis=0))
```
